```python
import jax
import jax.numpy as jnp
from jax import lax
import numpy as np

D_MODEL = 4096
BATCH = 1
SEQ = 8192
DEPTH = 2

GRID_W = 64
CTX_LEN = 256
BLOCK = 128
CHUNK = 128
WINDOW = 128
ROPE_THETA = 10000.0
EPS = 1e-6
NEG_INF = -1e30
N_MOD = 6

A_GROUPS = 8
A_GROUP_DIM = D_MODEL // 16
A_W = A_GROUPS * A_GROUP_DIM
B_HEAD_DIM = 64
B_HEADS = (D_MODEL // 2) // B_HEAD_DIM
B_KV_HEADS = B_HEADS // 8
B_GROUP = B_HEADS // B_KV_HEADS
B_QW = B_HEADS * B_HEAD_DIM
B_KW = B_KV_HEADS * B_HEAD_DIM
B_SCALE = B_HEAD_DIM ** -0.5
AB_KV0 = 2 * A_W + B_QW
AB_COLS = AB_KV0 + 2 * B_KW
AB_OUT = A_W + B_QW
C_HEAD_V = 128
C_HEADS = (3 * D_MODEL // 4) // C_HEAD_V
C_Q_RANK = 3 * D_MODEL // 16
C_KV_RANK = D_MODEL // 8
C_NOPE = 128
C_ROPE = 64
C_SCALE = (C_NOPE + C_ROPE) ** -0.5
D_GROUPS = 4
D_GROUP_DIM = D_MODEL // 16
D_W = D_GROUPS * D_GROUP_DIM
CD_KV0 = C_Q_RANK
CD_D0 = C_Q_RANK + C_KV_RANK + C_ROPE
CD_COLS = CD_D0 + D_W
CD_OUT = C_HEADS * C_HEAD_V + D_W
N_EXPERTS = 32
TOP_K = 4
D_EXPERT = 3 * D_MODEL // 16
SWIGLU_LIMIT = 7.0
SWIGLU_ALPHA = 1.702

kernel_name = 'hybrid_diffusion_gmlp_swa_mla_fnet_moe'


def rmsnorm(x, g):
    xf = x.astype(jnp.float32)
    y = xf * lax.rsqrt(jnp.mean(xf * xf, axis=-1, keepdims=True) + EPS)
    return (y * g.astype(jnp.float32)).astype(x.dtype)


def modulate(h, shift, scale):
    return h * (1 + scale) + shift


def axial_rope_tables(n_tokens, rot_dim):
    rows = n_tokens // GRID_W
    row = jnp.repeat(jnp.arange(rows, dtype=jnp.float32), GRID_W)
    col = jnp.tile(jnp.arange(GRID_W, dtype=jnp.float32), rows)
    n_freq = rot_dim // 4
    inv_freq = ROPE_THETA ** (-jnp.arange(n_freq, dtype=jnp.float32) / n_freq)
    ang = jnp.concatenate([row[:, None] * inv_freq, col[:, None] * inv_freq], axis=-1)
    return jnp.cos(ang), jnp.sin(ang)


def apply_rope(x, cos, sin):
    half = x.shape[-1] // 2
    xf = x.astype(jnp.float32)
    x1, x2 = xf[..., :half], xf[..., half:]
    cs, sn = cos[:, None, :], sin[:, None, :]
    return jnp.concatenate([x1 * cs - x2 * sn, x2 * cs + x1 * sn], axis=-1).astype(x.dtype)


def chunk_gmlp(u, v, ws, bs):
    bsz, n, _ = u.shape
    ug = jax.nn.gelu(u).reshape(bsz, n, A_GROUPS, A_GROUP_DIM)
    vf = jax.nn.gelu(v).astype(jnp.float32).reshape(bsz, n, A_GROUPS, A_GROUP_DIM)
    mu = jnp.mean(vf, axis=-1, keepdims=True)
    var = jnp.mean(jnp.square(vf - mu), axis=-1, keepdims=True)
    vn = ((vf - mu) * lax.rsqrt(var + EPS)).astype(u.dtype)
    vc = vn.reshape(bsz, n // CHUNK, CHUNK, A_GROUPS, A_GROUP_DIM)
    mixed = jnp.einsum('gpq,bnqgc->bnpgc', ws, vc) + bs.T[None, None, :, :, None]
    return (ug * mixed.reshape(bsz, n, A_GROUPS, A_GROUP_DIM)).reshape(bsz, n, A_W)


def sink_column(sink, score_shape):
    s = sink.astype(jnp.float32).reshape(B_KV_HEADS, B_GROUP)[None, :, :, None, None]
    return jnp.broadcast_to(s, score_shape[:-1] + (1,))


def context_gqa_sink(q, k, v, sink):
    bsz, n, _, _ = q.shape
    qg = q.reshape(bsz, n, B_KV_HEADS, B_GROUP, B_HEAD_DIM)
    s = jnp.einsum('bqhgd,bkhd->bhgqk', qg, k).astype(jnp.float32) * B_SCALE
    p = jax.nn.softmax(jnp.concatenate([s, sink_column(sink, s.shape)], axis=-1), axis=-1)[..., :k.shape[1]]
    o = jnp.einsum('bhgqk,bkhd->bqhgd', p.astype(v.dtype), v)
    return o.reshape(bsz, n, B_QW)


def window_gqa_sink(q, k, v, k_ctx, v_ctx, sink):
    bsz, n, _, _ = q.shape
    nb = n // BLOCK
    n_ctx = k_ctx.shape[1]
    span = 3 * BLOCK
    qb = jnp.moveaxis(q.reshape(bsz, nb, BLOCK, B_KV_HEADS, B_GROUP, B_HEAD_DIM), 1, 0)
    pad = ((0, 0), (BLOCK, BLOCK), (0, 0), (0, 0))
    kp, vp = jnp.pad(k, pad), jnp.pad(v, pad)
    rel = jnp.arange(span)[None, :] - BLOCK - jnp.arange(BLOCK)[:, None]
    in_window = jnp.abs(rel) <= WINDOW

    def one_block(args):
        i, qi = args
        start = i * BLOCK
        kw = lax.dynamic_slice_in_dim(kp, start, span, axis=1)
        vw = lax.dynamic_slice_in_dim(vp, start, span, axis=1)
        k_pos = start - BLOCK + jnp.arange(span)
        valid = in_window & ((k_pos >= 0) & (k_pos < n))[None, :]
        s_win = jnp.einsum('bqhgd,bkhd->bhgqk', qi, kw).astype(jnp.float32) * B_SCALE
        s_win = jnp.where(valid, s_win, NEG_INF)
        s_ctx = jnp.einsum('bqhgd,bkhd->bhgqk', qi, k_ctx).astype(jnp.float32) * B_SCALE
        logits = jnp.concatenate([s_win, s_ctx, sink_column(sink, s_win.shape)], axis=-1)
        p = jax.nn.softmax(logits, axis=-1).astype(v.dtype)
        return (jnp.einsum('bhgqk,bkhd->bqhgd', p[..., :span], vw)
                + jnp.einsum('bhgqk,bkhd->bqhgd', p[..., span:span + n_ctx], v_ctx))

    o = lax.map(one_block, (jnp.arange(nb), qb))
    return jnp.moveaxis(o, 0, 1).reshape(bsz, n, B_QW)


def mla_attend(q_nope, q_rope, k_nope, k_rope, v):
    s = (jnp.einsum('bqhd,bkhd->bhqk', q_nope, k_nope)
         + jnp.einsum('bqhr,bkr->bhqk', q_rope, k_rope)).astype(jnp.float32) * C_SCALE
    p = jax.nn.softmax(s, axis=-1).astype(v.dtype)
    return jnp.einsum('bhqk,bkhd->bqhd', p, v)


def fourier_mix(z):
    bsz, n, _ = z.shape
    zg = z.astype(jnp.float32).reshape(bsz, n, D_GROUPS, D_GROUP_DIM)
    y = jnp.fft.fft2(zg, axes=(1, 3), norm='ortho').real
    return y.reshape(bsz, n, D_W).astype(z.dtype)


def ab_mixer(h_ctx, h_lat, w_in, ws, bs, sink, w_out, need_ctx):
    bsz, n, _ = h_lat.shape
    n_ctx = h_ctx.shape[1]
    cos, sin = axial_rope_tables(n, B_HEAD_DIM)

    def heads(t, length, h):
        return t.reshape(bsz, length, h, B_HEAD_DIM)

    z = h_lat @ w_in
    a_lat = chunk_gmlp(z[..., :A_W], z[..., A_W:2 * A_W], ws, bs)
    q = apply_rope(heads(z[..., 2 * A_W:AB_KV0], n, B_HEADS), cos, sin)
    k = apply_rope(heads(z[..., AB_KV0:AB_KV0 + B_KW], n, B_KV_HEADS), cos, sin)
    v = heads(z[..., AB_KV0 + B_KW:], n, B_KV_HEADS)
    zc = h_ctx @ w_in[:, AB_KV0:]
    k_c = heads(zc[..., :B_KW], n_ctx, B_KV_HEADS)
    v_c = heads(zc[..., B_KW:], n_ctx, B_KV_HEADS)
    b_lat = window_gqa_sink(q, k, v, k_c, v_c, sink)
    y_lat = jnp.concatenate([a_lat, b_lat], axis=-1) @ w_out
    y_ctx = None
    if need_ctx:
        zq = h_ctx @ w_in[:, :AB_KV0]
        a_ctx = chunk_gmlp(zq[..., :A_W], zq[..., A_W:2 * A_W], ws, bs)
        b_ctx = context_gqa_sink(heads(zq[..., 2 * A_W:], n_ctx, B_HEADS), k_c, v_c, sink)
        y_ctx = jnp.concatenate([a_ctx, b_ctx], axis=-1) @ w_out
    return y_ctx, y_lat


def cd_mixer(h_ctx, h_lat, w_in, g_cq, w_uq, g_ckv, w_ukv, w_out, need_ctx):
    bsz, n, _ = h_lat.shape
    n_ctx = h_ctx.shape[1]
    cos, sin = axial_rope_tables(n, C_ROPE)

    def q_heads(cq, length):
        qh = (rmsnorm(cq, g_cq) @ w_uq).reshape(bsz, length, C_HEADS, C_NOPE + C_ROPE)
        return qh[..., :C_NOPE], qh[..., C_NOPE:]

    def kv_heads(ckv_kr, length):
        ckv, kr = ckv_kr[..., :C_KV_RANK], ckv_kr[..., C_KV_RANK:]
        kvh = (rmsnorm(ckv, g_ckv) @ w_ukv).reshape(bsz, length, C_HEADS, C_NOPE + C_HEAD_V)
        return kvh[..., :C_NOPE], kvh[..., C_NOPE:], kr

    z = h_lat @ w_in
    qn, qr = q_heads(z[..., :CD_KV0], n)
    qr = apply_rope(qr, cos, sin)
    kn, v, kr = kv_heads(z[..., CD_KV0:CD_D0], n)
    kr = apply_rope(kr[:, :, None, :], cos, sin)[:, :, 0, :]
    kn_c, v_c, kr_c = kv_heads(h_ctx @ w_in[:, CD_KV0:CD_D0], n_ctx)
    kn_all = jnp.concatenate([kn, kn_c], axis=1)
    kr_all = jnp.concatenate([kr, kr_c], axis=1)
    v_all = jnp.concatenate([v, v_c], axis=1)
    nb = n // BLOCK

    def blocks(t):
        return jnp.moveaxis(t.reshape(bsz, nb, BLOCK, *t.shape[2:]), 1, 0)

    o = lax.map(lambda qs: mla_attend(qs[0], qs[1], kn_all, kr_all, v_all), (blocks(qn), blocks(qr)))
    c_lat = jnp.moveaxis(o, 0, 1).reshape(bsz, n, C_HEADS * C_HEAD_V)
    d_lat = fourier_mix(z[..., CD_D0:])
    y_lat = jnp.concatenate([c_lat, d_lat], axis=-1) @ w_out
    y_ctx = None
    if need_ctx:
        qn_c, qr_c = q_heads(h_ctx @ w_in[:, :CD_KV0], n_ctx)
        c_ctx_o = mla_attend(qn_c, qr_c, kn_c, kr_c, v_c).reshape(bsz, n_ctx, C_HEADS * C_HEAD_V)
        d_ctx = fourier_mix(h_ctx @ w_in[:, CD_D0:])
        y_ctx = jnp.concatenate([c_ctx_o, d_ctx], axis=-1) @ w_out
    return y_ctx, y_lat


def moe_ffn(h, w_router, b_router, w_gu, b_gu, w_down, b_down):
    logits = (h @ w_router + b_router).astype(jnp.float32)
    top_v, top_i = lax.top_k(logits, TOP_K)
    top_w = jax.nn.softmax(top_v, axis=-1)
    gates = jnp.sum(jax.nn.one_hot(top_i, N_EXPERTS, dtype=jnp.float32) * top_w[..., None], axis=-2)
    gates = gates.astype(h.dtype)

    def expert(acc, params):
        wgu, bgu, wd, bd, g = params
        gu = h @ wgu + bgu
        gate = jnp.minimum(gu[:, :D_EXPERT], SWIGLU_LIMIT)
        up = jnp.clip(gu[:, D_EXPERT:], -SWIGLU_LIMIT, SWIGLU_LIMIT)
        act = (up + 1) * (gate * jax.nn.sigmoid(SWIGLU_ALPHA * gate))
        return acc + g[:, None] * (act @ wd + bd), None

    out, _ = lax.scan(expert, jnp.zeros_like(h), (w_gu, b_gu, w_down, b_down, gates.T))
    return out


def setup_inputs(seed: int = 0) -> dict:
    key = jax.random.key(seed)
    k = jax.random.split(key, 26)
    f32 = jnp.float32
    n_ab = (DEPTH + 1) // 2
    n_cd = DEPTH // 2

    def normal(i, shape, scale):
        return jax.random.normal(k[i], shape, f32) * scale

    def gain(i, shape):
        return 1.0 + normal(i, shape, 0.02)

    return {
        'x': normal(0, (BATCH, SEQ, D_MODEL), 1.0),
        'c': normal(1, (BATCH, D_MODEL), 1.0),
        'ctx': normal(2, (BATCH, CTX_LEN, D_MODEL), 1.0),
        'c_ctx': normal(3, (D_MODEL,), 1.0),
        'w_mod': normal(4, (DEPTH, D_MODEL, N_MOD * D_MODEL), 0.5 * D_MODEL ** -0.5),
        'b_mod': normal(5, (DEPTH, N_MOD * D_MODEL), 0.02),
        'g_mix': gain(6, (DEPTH, D_MODEL)),
        'g_ffn': gain(7, (DEPTH, D_MODEL)),
        'w_in_ab': normal(8, (n_ab, D_MODEL, AB_COLS), D_MODEL ** -0.5),
        'ws_a': normal(9, (n_ab, A_GROUPS, CHUNK, CHUNK), CHUNK ** -0.5),
        'bs_a': gain(10, (n_ab, A_GROUPS, CHUNK)),
        'sink_b': normal(11, (n_ab, B_HEADS), 0.1),
        'w_out_ab': normal(12, (n_ab, AB_OUT, D_MODEL), AB_OUT ** -0.5),
        'w_in_cd': normal(13, (n_cd, D_MODEL, CD_COLS), D_MODEL ** -0.5),
        'g_cq': gain(14, (n_cd, C_Q_RANK)),
        'w_uq': normal(15, (n_cd, C_Q_RANK, C_HEADS * (C_NOPE + C_ROPE)), C_Q_RANK ** -0.5),
        'g_ckv': gain(16, (n_cd, C_KV_RANK)),
        'w_ukv': normal(17, (n_cd, C_KV_RANK, C_HEADS * (C_NOPE + C_HEAD_V)), C_KV_RANK ** -0.5),
        'w_out_cd': normal(18, (n_cd, CD_OUT, D_MODEL), CD_OUT ** -0.5),
        'w_router': normal(19, (DEPTH, D_MODEL, N_EXPERTS), D_MODEL ** -0.5),
        'b_router': normal(20, (DEPTH, N_EXPERTS), 0.01),
        'w_gu': normal(21, (DEPTH, N_EXPERTS, D_MODEL, 2 * D_EXPERT), D_MODEL ** -0.5),
        'b_gu': normal(22, (DEPTH, N_EXPERTS, 2 * D_EXPERT), 0.02),
        'w_down': normal(23, (DEPTH, N_EXPERTS, D_EXPERT, D_MODEL), D_EXPERT ** -0.5),
        'b_down': normal(24, (DEPTH, N_EXPERTS, D_MODEL), 0.02),
        'g_final': gain(25, (D_MODEL,)),
    }


def reference(x, c, ctx, c_ctx, w_mod, b_mod, g_mix, g_ffn, w_in_ab, ws_a, bs_a, sink_b, w_out_ab,
              w_in_cd, g_cq, w_uq, g_ckv, w_ukv, w_out_cd, w_router, b_router, w_gu, b_gu, w_down,
              b_down, g_final):
    bsz, n, d = x.shape
    n_ctx = ctx.shape[1]
    silu_c = jax.nn.silu(c)
    silu_cc = jax.nn.silu(c_ctx)
    for l in range(DEPTH):
        need_ctx = l < DEPTH - 1
        mod = (silu_c @ w_mod[l] + b_mod[l]).reshape(bsz, N_MOD, 1, d)
        mod_c = (silu_cc @ w_mod[l] + b_mod[l]).reshape(N_MOD, d)
        h = modulate(rmsnorm(x, g_mix[l]), mod[:, 0], mod[:, 1])
        h_c = modulate(rmsnorm(ctx, g_mix[l]), mod_c[0], mod_c[1])
        j = l // 2
        if l % 2 == 0:
            y_c, y = ab_mixer(h_c, h, w_in_ab[j], ws_a[j], bs_a[j], sink_b[j], w_out_ab[j], need_ctx)
        else:
            y_c, y = cd_mixer(h_c, h, w_in_cd[j], g_cq[j], w_uq[j], g_ckv[j], w_ukv[j], w_out_cd[j], need_ctx)
        x = x + mod[:, 2] * y
        if need_ctx:
            ctx = ctx + mod_c[2] * y_c
            hf = jnp.concatenate([modulate(rmsnorm(ctx, g_ffn[l]), mod_c[3], mod_c[4]),
                                  modulate(rmsnorm(x, g_ffn[l]), mod[:, 3], mod[:, 4])], axis=1)
            f = moe_ffn(hf.reshape(-1, d), w_router[l], b_router[l], w_gu[l], b_gu[l],
                        w_down[l], b_down[l]).reshape(bsz, n_ctx + n, d)
            ctx = ctx + mod_c[5] * f[:, :n_ctx]
            x = x + mod[:, 5] * f[:, n_ctx:]
        else:
            hf = modulate(rmsnorm(x, g_ffn[l]), mod[:, 3], mod[:, 4])
            f = moe_ffn(hf.reshape(-1, d), w_router[l], b_router[l], w_gu[l], b_gu[l],
                        w_down[l], b_down[l]).reshape(bsz, n, d)
            x = x + mod[:, 5] * f
    return rmsnorm(x, g_final)
```

```python
import functools
import math

import jax
import jax.numpy as jnp
from jax import lax
from jax.experimental import pallas as pl
from jax.experimental.pallas import tpu as pltpu

F32 = jnp.float32
BF16 = jnp.bfloat16

D_MODEL = 4096
GRID_W = 64
BLOCK = 128
CHUNK = 128
ROPE_THETA = 10000.0
EPS = 1e-6
NEG_INF = -1e30
N_MOD = 6
A_GROUPS = 8
A_GROUP_DIM = D_MODEL // 16
A_W = A_GROUPS * A_GROUP_DIM
B_HEAD_DIM = 64
B_HEADS = (D_MODEL // 2) // B_HEAD_DIM
B_KV_HEADS = B_HEADS // 8
B_GROUP = B_HEADS // B_KV_HEADS
B_QW = B_HEADS * B_HEAD_DIM
B_KW = B_KV_HEADS * B_HEAD_DIM
B_SCALE = B_HEAD_DIM ** -0.5
AB_KV0 = 2 * A_W + B_QW
C_HEAD_V = 128
C_HEADS = (3 * D_MODEL // 4) // C_HEAD_V
C_Q_RANK = 3 * D_MODEL // 16
C_KV_RANK = D_MODEL // 8
C_NOPE = 128
C_ROPE = 64
C_SCALE = (C_NOPE + C_ROPE) ** -0.5
C_QK_PAD = 256
D_GROUPS = 4
D_GROUP_DIM = D_MODEL // 16
D_W = D_GROUPS * D_GROUP_DIM
CD_KV0 = C_Q_RANK
CD_D0 = C_Q_RANK + C_KV_RANK + C_ROPE
N_EXPERTS = 32
TOP_K = 4
D_EXPERT = 3 * D_MODEL // 16
SWIGLU_LIMIT = 7.0
SWIGLU_ALPHA = 1.702

V7X_VMEM_BYTES = 64 * 1024 * 1024
VMEM_LIMIT_BYTES = V7X_VMEM_BYTES - 8 * 1024 * 1024
LANES = 128

ROW_TILE = 256
MOE_TM = 256
COMBINE_TB = 128
MLA_TQ = 512
MLA_TK = 1024
FFT_A = 64
FFT_B = 128


def _cparams(*sem):
    return pltpu.CompilerParams(dimension_semantics=sem, vmem_limit_bytes=VMEM_LIMIT_BYTES)


def _mm_body(n_a, has_bias, has_res, *refs):
    a_refs = refs[:n_a]
    w_refs = refs[n_a:2 * n_a]
    pos = 2 * n_a
    acc = None
    for a_ref, w_ref in zip(a_refs, w_refs):
        d = jnp.dot(a_ref[...], w_ref[...], preferred_element_type=F32)
        acc = d if acc is None else acc + d
    if has_bias:
        acc = acc + refs[pos][...]
        pos += 1
    if has_res:
        acc = refs[pos][...] + refs[pos + 1][...] * acc
    o_ref = refs[-1]
    o_ref[...] = acc.astype(o_ref.dtype)


def matmul(a_list, w_list, *, out_dtype, tm, tn, bias=None, res=None, gate=None, a_resident=False, name="mm"):
    m = a_list[0].shape[0]
    n = w_list[0].shape[1]
    assert m % tm == 0 and n % tn == 0, (m, tm, n, tn)
    if a_resident:
        grid = (m // tm, n // tn)
        im = lambda i, j: i
        jn = lambda i, j: j
    else:
        grid = (n // tn, m // tm)
        im = lambda j, i: i
        jn = lambda j, i: j
    in_specs = []
    for a in a_list:
        in_specs.append(pl.BlockSpec((tm, a.shape[1]), lambda *g: (im(*g), 0)))
    for w in w_list:
        in_specs.append(pl.BlockSpec((w.shape[0], tn), lambda *g: (0, jn(*g))))
    args = list(a_list) + list(w_list)
    if bias is not None:
        in_specs.append(pl.BlockSpec((1, tn), lambda *g: (0, jn(*g))))
        args.append(bias)
    if res is not None:
        in_specs.append(pl.BlockSpec((tm, tn), lambda *g: (im(*g), jn(*g))))
        in_specs.append(pl.BlockSpec((1, tn), lambda *g: (0, jn(*g))))
        args += [res, gate]
    return pl.pallas_call(
        functools.partial(_mm_body, len(a_list), bias is not None, res is not None),
        grid=grid,
        in_specs=in_specs,
        out_specs=pl.BlockSpec((tm, tn), lambda *g: (im(*g), jn(*g))),
        out_shape=jax.ShapeDtypeStruct((m, n), out_dtype),
        compiler_params=_cparams("parallel", "parallel"),
        name=name,
    )(*args)


def _mod_body(c_ref, w_ref, b_ref, o_ref):
    c = c_ref[...]
    a = (c * jax.nn.sigmoid(c)).astype(BF16)
    o_ref[...] = jnp.dot(a, w_ref[...].astype(BF16), preferred_element_type=F32) + b_ref[...]


def modulation(c_rows, w_mod_l, b_mod_l, tn=1024):
    d, n = w_mod_l.shape
    return pl.pallas_call(
        _mod_body,
        grid=(n // tn,),
        in_specs=[
            pl.BlockSpec((8, d), lambda j: (0, 0)),
            pl.BlockSpec((d, tn), lambda j: (0, j)),
            pl.BlockSpec((1, tn), lambda j: (0, j)),
        ],
        out_specs=pl.BlockSpec((8, tn), lambda j: (0, j)),
        out_shape=jax.ShapeDtypeStruct((8, n), F32),
        compiler_params=_cparams("parallel"),
        name="modulation",
    )(c_rows, w_mod_l, b_mod_l[None])


def _norm_mod_body(x_ref, g_ref, scale_ref, shift_ref, o_ref):
    x = x_ref[...].astype(F32)
    y = x * lax.rsqrt(jnp.mean(x * x, axis=-1, keepdims=True) + EPS)
    y = y * g_ref[...]
    o_ref[...] = (y * (1.0 + scale_ref[0]) + shift_ref[0]).astype(o_ref.dtype)


def norm_mod(x, g, scale_tbl, shift_tbl, *, out_dtype, rows=None, split_tile=None, tm=ROW_TILE, name="norm_mod"):
    d = x.shape[1]
    rows = x.shape[0] if rows is None else rows
    assert rows % tm == 0
    if split_tile is None:
        sel = lambda i: (0, 0, 0)
    else:
        sel = lambda i: (jnp.where(i >= split_tile, 1, 0), 0, 0)
    return pl.pallas_call(
        _norm_mod_body,
        grid=(rows // tm,),
        in_specs=[
            pl.BlockSpec((tm, d), lambda i: (i, 0)),
            pl.BlockSpec((1, d), lambda i: (0, 0)),
            pl.BlockSpec((1, 1, d), sel),
            pl.BlockSpec((1, 1, d), sel),
        ],
        out_specs=pl.BlockSpec((tm, d), lambda i: (i, 0)),
        out_shape=jax.ShapeDtypeStruct((rows, d), out_dtype),
        compiler_params=_cparams("parallel"),
        name=name,
    )(x, g[None].astype(F32), scale_tbl, shift_tbl)


def _gmlp_body(u_ref, v_ref, ws_ref, bs_ref, o_ref):
    for g in range(A_GROUPS):
        sl = slice(g * A_GROUP_DIM, (g + 1) * A_GROUP_DIM)
        ug = jax.nn.gelu(u_ref[:, sl].astype(F32))
        vf = jax.nn.gelu(v_ref[:, sl].astype(F32))
        mu = jnp.mean(vf, axis=-1, keepdims=True)
        vc = vf - mu
        var = jnp.mean(vc * vc, axis=-1, keepdims=True)
        vn = (vc * lax.rsqrt(var + EPS)).astype(BF16)
        mixed = jnp.dot(ws_ref[g], vn, preferred_element_type=F32) + bs_ref[g]
        o_ref[:, sl] = (ug * mixed).astype(o_ref.dtype)


def gmlp(z, ws_bf, bs_col):
    n = z.shape[0]
    return pl.pallas_call(
        _gmlp_body,
        grid=(n // CHUNK,),
        in_specs=[
            pl.BlockSpec((CHUNK, A_W), lambda i: (i, 0)),
            pl.BlockSpec((CHUNK, A_W), lambda i: (i, 1)),
            pl.BlockSpec((A_GROUPS, CHUNK, CHUNK), lambda i: (0, 0, 0)),
            pl.BlockSpec((A_GROUPS, CHUNK, 1), lambda i: (0, 0, 0)),
        ],
        out_specs=pl.BlockSpec((CHUNK, A_W), lambda i: (i, 0)),
        out_shape=jax.ShapeDtypeStruct((n, A_W), BF16),
        compiler_params=_cparams("parallel"),
        name="gmlp",
    )(z, z, ws_bf, bs_col)


def _swap_halves(x):
    lane = lax.broadcasted_iota(jnp.int32, x.shape, 1)
    return jnp.where((lane & 63) < 32, pltpu.roll(x, 96, 1), pltpu.roll(x, 32, 1))


def _rope_slab(x, cos, sin):
    return x * cos + _swap_halves(x) * sin


def _rope_q_body(z_ref, cos_ref, sin_ref, o_ref):
    cos = cos_ref[...]
    sin = sin_ref[...]
    for s in range(z_ref.shape[1] // LANES):
        sl = slice(s * LANES, (s + 1) * LANES)
        x = z_ref[:, sl].astype(F32)
        o_ref[:, sl] = (_rope_slab(x, cos, sin) * B_SCALE).astype(o_ref.dtype)


def rope_q(z, cos_t, sin_t, tm=ROW_TILE):
    n = z.shape[0]
    wb = 256
    col0 = (2 * A_W) // wb
    return pl.pallas_call(
        _rope_q_body,
        grid=(n // tm, B_QW // wb),
        in_specs=[
            pl.BlockSpec((tm, wb), lambda i, j: (i, col0 + j)),
            pl.BlockSpec((tm, LANES), lambda i, j: (i, 0)),
            pl.BlockSpec((tm, LANES), lambda i, j: (i, 0)),
        ],
        out_specs=pl.BlockSpec((tm, wb), lambda i, j: (i, j)),
        out_shape=jax.ShapeDtypeStruct((n, B_QW), BF16),
        compiler_params=_cparams("parallel", "parallel"),
        name="rope_q",
    )(z, cos_t, sin_t)


def _dup_heads(x):
    lane = lax.broadcasted_iota(jnp.int32, x.shape, 1)
    r = pltpu.roll(x, 64, 1)
    lo = lane < 64
    return jnp.where(lo, x, r), jnp.where(lo, r, x)


def _rope_kv_body(z_ref, cos_ref, sin_ref, kd_ref, vd_ref):
    cos = cos_ref[...]
    sin = sin_ref[...]
    for s in range(B_KW // LANES):
        xk = z_ref[:, s * LANES:(s + 1) * LANES].astype(F32)
        e, o = _dup_heads(_rope_slab(xk, cos, sin))
        kd_ref[:, (2 * s) * LANES:(2 * s + 1) * LANES] = e.astype(kd_ref.dtype)
        kd_ref[:, (2 * s + 1) * LANES:(2 * s + 2) * LANES] = o.astype(kd_ref.dtype)
        xv = z_ref[:, B_KW + s * LANES:B_KW + (s + 1) * LANES].astype(F32)
        e, o = _dup_heads(xv)
        vd_ref[:, (2 * s) * LANES:(2 * s + 1) * LANES] = e.astype(vd_ref.dtype)
        vd_ref[:, (2 * s + 1) * LANES:(2 * s + 2) * LANES] = o.astype(vd_ref.dtype)


def rope_kv(z, cos_t, sin_t, tm=ROW_TILE):
    n = z.shape[0]
    wb = 2 * B_KW
    col0 = AB_KV0 // wb
    out_w = B_KV_HEADS * LANES
    return pl.pallas_call(
        _rope_kv_body,
        grid=(n // tm,),
        in_specs=[
            pl.BlockSpec((tm, wb), lambda i: (i, col0)),
            pl.BlockSpec((tm, LANES), lambda i: (i, 0)),
            pl.BlockSpec((tm, LANES), lambda i: (i, 0)),
        ],
        out_specs=[pl.BlockSpec((tm, out_w), lambda i: (i, 0)), pl.BlockSpec((tm, out_w), lambda i: (i, 0))],
        out_shape=[jax.ShapeDtypeStruct((n, out_w), BF16), jax.ShapeDtypeStruct((n, out_w), BF16)],
        compiler_params=_cparams("parallel"),
        name="rope_kv",
    )(z, cos_t, sin_t)


def _window_attn_body(has_window, nb, *refs):
    if has_window:
        q_ref, kp_ref, kc_ref, kn_ref, vp_ref, vc_ref, vn_ref, kx_ref, vx_ref, sink_ref, o_ref = refs
    else:
        q_ref, kx_ref, vx_ref, sink_ref, o_ref = refs
    i = pl.program_id(0)
    n_ctx = kx_ref.shape[0]
    lane = lax.broadcasted_iota(jnp.int32, (BLOCK, LANES), 1)
    lo = lane < 64
    if has_window:
        r = lax.broadcasted_iota(jnp.int32, (BLOCK, BLOCK), 0)
        c = lax.broadcasted_iota(jnp.int32, (BLOCK, BLOCK), 1)
        ones = jnp.ones((BLOCK, BLOCK), jnp.bool_)
        valid = jnp.concatenate(
            [(c >= r) & (i > 0), ones, (c <= r) & (i < nb - 1), jnp.ones((BLOCK, n_ctx), jnp.bool_)], axis=1)
    for j in range(B_KV_HEADS):
        hs = slice(j * LANES, (j + 1) * LANES)
        if has_window:
            kj = jnp.concatenate([kp_ref[:, hs], kc_ref[:, hs], kn_ref[:, hs], kx_ref[:, hs]], axis=0)
            vj = jnp.concatenate([vp_ref[:, hs], vc_ref[:, hs], vn_ref[:, hs], vx_ref[:, hs]], axis=0)
        else:
            kj = kx_ref[:, hs]
            vj = vx_ref[:, hs]
        rows = []
        for m in range(B_GROUP // 2):
            slab = q_ref[:, (j * (B_GROUP // 2) + m) * LANES:(j * (B_GROUP // 2) + m + 1) * LANES]
            zero = jnp.zeros_like(slab)
            rows += [jnp.where(lo, slab, zero), jnp.where(lo, zero, slab)]
        qs = jnp.concatenate(rows, axis=0)
        s = lax.dot_general(qs, kj, (((1,), (1,)), ((), ())), preferred_element_type=F32)
        s = s.reshape(B_GROUP, BLOCK, kj.shape[0])
        if has_window:
            s = jnp.where(valid[None], s, NEG_INF)
        sink = sink_ref[j]
        mx = jnp.maximum(jnp.max(s, axis=-1, keepdims=True), sink)
        p = jnp.exp(s - mx)
        denom = jnp.sum(p, axis=-1, keepdims=True) + jnp.exp(sink - mx)
        o = jnp.dot(p.reshape(B_GROUP * BLOCK, kj.shape[0]).astype(BF16), vj, preferred_element_type=F32)
        o = o.reshape(B_GROUP, BLOCK, LANES) / denom
        for m in range(B_GROUP // 2):
            col = (j * (B_GROUP // 2) + m) * LANES
            o_ref[:, col:col + LANES] = jnp.where(lo, o[2 * m], o[2 * m + 1]).astype(o_ref.dtype)


def window_attn(q, kd, vd, kd_ctx, vd_ctx, sink4, has_window):
    n = q.shape[0]
    nb = n // BLOCK
    n_ctx = kd_ctx.shape[0]
    kw = B_KV_HEADS * LANES
    blk = lambda f: pl.BlockSpec((BLOCK, kw), f)
    full_ctx = pl.BlockSpec((n_ctx, kw), lambda i: (0, 0))
    in_specs = [pl.BlockSpec((BLOCK, B_QW), lambda i: (i, 0))]
    args = [q]
    if has_window:
        prev = lambda i: (jnp.maximum(i - 1, 0), 0)
        cur = lambda i: (i, 0)
        nxt = lambda i: (jnp.minimum(i + 1, nb - 1), 0)
        in_specs += [blk(prev), blk(cur), blk(nxt), blk(prev), blk(cur), blk(nxt)]
        args += [kd, kd, kd, vd, vd, vd]
    in_specs += [full_ctx, full_ctx, pl.BlockSpec((B_KV_HEADS, B_GROUP, 1, 1), lambda i: (0, 0, 0, 0))]
    args += [kd_ctx, vd_ctx, sink4]
    return pl.pallas_call(
        functools.partial(_window_attn_body, has_window, nb),
        grid=(nb,),
        in_specs=in_specs,
        out_specs=pl.BlockSpec((BLOCK, B_QW), lambda i: (i, 0)),
        out_shape=jax.ShapeDtypeStruct((n, B_QW), BF16),
        compiler_params=_cparams("parallel"),
        name="window_attn" if has_window else "ctx_attn",
    )(*args)


def _mla_rope(u, cos, sin):
    return u * cos + _swap_halves(u) * sin


def _kvprep_body(z_ref, g_ref, cos_ref, sin_ref, o_ref):
    ckv = z_ref[:, :C_KV_RANK].astype(F32)
    y = ckv * lax.rsqrt(jnp.mean(ckv * ckv, axis=-1, keepdims=True) + EPS) * g_ref[...]
    o_ref[:, :C_KV_RANK] = y.astype(o_ref.dtype)
    u = z_ref[:, C_KV_RANK:].astype(F32)
    o_ref[:, C_KV_RANK:] = _mla_rope(u, cos_ref[...], sin_ref[...]).astype(o_ref.dtype)


def kvprep(ckvkr, g_ckv, cos_t, sin_t, tm=ROW_TILE):
    n, w = ckvkr.shape
    return pl.pallas_call(
        _kvprep_body,
        grid=(n // tm,),
        in_specs=[
            pl.BlockSpec((tm, w), lambda i: (i, 0)),
            pl.BlockSpec((1, C_KV_RANK), lambda i: (0, 0)),
            pl.BlockSpec((tm, LANES), lambda i: (i, 0)),
            pl.BlockSpec((tm, LANES), lambda i: (i, 0)),
        ],
        out_specs=pl.BlockSpec((tm, w), lambda i: (i, 0)),
        out_shape=jax.ShapeDtypeStruct((n, w), BF16),
        compiler_params=_cparams("parallel"),
        name="kvprep",
    )(ckvkr, g_ckv[None].astype(F32), cos_t, sin_t)


def _mla_body(tk, q_ref, cos_ref, sin_ref, k_ref, v_ref, kx_ref, vx_ref, o_ref):
    tq = q_ref.shape[0]
    qn = q_ref[:, :LANES].astype(F32) * C_SCALE
    qr = _mla_rope(q_ref[:, LANES:].astype(F32), cos_ref[...], sin_ref[...]) * C_SCALE
    q = jnp.concatenate([qn, qr], axis=1).astype(BF16)

    def step(k, v, carry):
        m, l, acc = carry
        s = lax.dot_general(q, k, (((1,), (1,)), ((), ())), preferred_element_type=F32)
        m_new = jnp.maximum(m, jnp.max(s, axis=-1, keepdims=True))
        alpha = jnp.exp(m - m_new)
        p = jnp.exp(s - m_new)
        l = alpha * l + jnp.sum(p, axis=-1, keepdims=True)
        acc = alpha * acc + jnp.dot(p.astype(BF16), v, preferred_element_type=F32)
        return m_new, l, acc

    def chunk(c, carry):
        off = pl.multiple_of(c * tk, tk)
        return step(k_ref[pl.ds(off, tk), :], v_ref[pl.ds(off, tk), :], carry)

    init = (jnp.full((tq, 1), -jnp.inf, F32), jnp.zeros((tq, 1), F32), jnp.zeros((tq, C_HEAD_V), F32))
    carry = lax.fori_loop(0, k_ref.shape[0] // tk, chunk, init)
    m, l, acc = step(kx_ref[...], vx_ref[...], carry)
    o_ref[...] = (acc / l).astype(o_ref.dtype)


def mla_attn(qp, kv, kv_ctx, cos_t, sin_t, tq=MLA_TQ, tk=MLA_TK):
    n = qp.shape[0]
    n_ctx = kv_ctx.shape[0]
    v_col0 = (C_HEADS * C_QK_PAD) // C_HEAD_V
    return pl.pallas_call(
        functools.partial(_mla_body, tk),
        grid=(C_HEADS, n // tq),
        in_specs=[
            pl.BlockSpec((tq, C_QK_PAD), lambda h, i: (i, h)),
            pl.BlockSpec((tq, LANES), lambda h, i: (i, 0)),
            pl.BlockSpec((tq, LANES), lambda h, i: (i, 0)),
            pl.BlockSpec((n, C_QK_PAD), lambda h, i: (0, h)),
            pl.BlockSpec((n, C_HEAD_V), lambda h, i: (0, v_col0 + h)),
            pl.BlockSpec((n_ctx, C_QK_PAD), lambda h, i: (0, h)),
            pl.BlockSpec((n_ctx, C_HEAD_V), lambda h, i: (0, v_col0 + h)),
        ],
        out_specs=pl.BlockSpec((tq, C_HEAD_V), lambda h, i: (i, h)),
        out_shape=jax.ShapeDtypeStruct((n, C_HEADS * C_HEAD_V), BF16),
        compiler_params=_cparams("parallel", "parallel"),
        name="mla_attn",
    )(qp, cos_t, sin_t, kv, kv, kv_ctx, kv_ctx)


_HI = lax.Precision.HIGHEST


def _fft1_body(x_ref, cs_ref, tr_ref, ti_ref, o_ref):
    u = jnp.dot(cs_ref[...], x_ref[...], precision=_HI, preferred_element_type=F32)
    ur, ui = u[:FFT_A], u[FFT_A:]
    reps = x_ref.shape[1] // LANES
    tr = jnp.concatenate([tr_ref[0]] * reps, axis=1)
    ti = jnp.concatenate([ti_ref[0]] * reps, axis=1)
    o_ref[0] = tr * ur - ti * ui
    o_ref[1] = tr * ui + ti * ur


def _fft2_body(v_ref, w2_ref, cc_ref, sc_ref, o_ref):
    vv = jnp.concatenate([v_ref[0, 0], v_ref[1, 0]], axis=0)
    g = jnp.dot(w2_ref[...], vv, precision=_HI, preferred_element_type=F32)
    gr, gi = g[:FFT_B], g[FFT_B:]
    for grp in range(D_GROUPS):
        sl = slice(grp * D_GROUP_DIM, (grp + 1) * D_GROUP_DIM)
        y = (jnp.dot(gr[:, sl], cc_ref[...], precision=_HI, preferred_element_type=F32)
             + jnp.dot(gi[:, sl], sc_ref[...], precision=_HI, preferred_element_type=F32))
        o_ref[:, sl] = y.astype(o_ref.dtype)


def _dft_tables(n):
    a_n, b_n = FFT_A, FFT_B
    assert a_n * b_n == n
    two_pi = 2.0 * math.pi

    def ang(i, j, period):
        return (two_pi / period) * ((i[:, None] * j[None, :]) % period).astype(F32)

    ia = jnp.arange(a_n, dtype=jnp.int32)
    ib = jnp.arange(b_n, dtype=jnp.int32)
    ic = jnp.arange(D_GROUP_DIM, dtype=jnp.int32)
    th_a = ang(ia, ia, a_n)
    cs = jnp.concatenate([jnp.cos(th_a), -jnp.sin(th_a)], axis=0)
    th_t = ang(ib, ia, n)
    tr = jnp.broadcast_to(jnp.cos(th_t)[:, :, None], (b_n, a_n, LANES))
    ti = jnp.broadcast_to(-jnp.sin(th_t)[:, :, None], (b_n, a_n, LANES))
    th_b = ang(ib, ib, b_n)
    cb, sb = jnp.cos(th_b), jnp.sin(th_b)
    w2 = jnp.concatenate([jnp.concatenate([cb, sb], axis=1), jnp.concatenate([-sb, cb], axis=1)], axis=0)
    th_c = ang(ic, ic, D_GROUP_DIM)
    norm = 1.0 / math.sqrt(n * D_GROUP_DIM)
    return cs, tr, ti, w2, jnp.cos(th_c) * norm, jnp.sin(th_c) * norm


def fourier_mix(dz):
    n = dz.shape[0]
    a_n, b_n = FFT_A, FFT_B
    cs, tr, ti, w2, cc, sc = _dft_tables(n)
    x2 = dz.reshape(a_n, b_n * D_W)
    v = pl.pallas_call(
        _fft1_body,
        grid=(b_n,),
        in_specs=[
            pl.BlockSpec((a_n, D_W), lambda b: (0, b)),
            pl.BlockSpec((2 * a_n, a_n), lambda b: (0, 0)),
            pl.BlockSpec((1, a_n, LANES), lambda b: (b, 0, 0)),
            pl.BlockSpec((1, a_n, LANES), lambda b: (b, 0, 0)),
        ],
        out_specs=pl.BlockSpec((2, a_n, D_W), lambda b: (0, 0, b)),
        out_shape=jax.ShapeDtypeStruct((2, a_n, b_n * D_W), F32),
        compiler_params=_cparams("parallel"),
        name="fft_stage1",
    )(x2, cs, tr, ti)
    v4 = v.reshape(2, a_n, b_n, D_W)
    y = pl.pallas_call(
        _fft2_body,
        grid=(a_n,),
        in_specs=[
            pl.BlockSpec((2, 1, b_n, D_W), lambda k: (0, k, 0, 0)),
            pl.BlockSpec((2 * b_n, 2 * b_n), lambda k: (0, 0)),
            pl.BlockSpec((D_GROUP_DIM, D_GROUP_DIM), lambda k: (0, 0)),
            pl.BlockSpec((D_GROUP_DIM, D_GROUP_DIM), lambda k: (0, 0)),
        ],
        out_specs=pl.BlockSpec((b_n, D_W), lambda k: (0, k)),
        out_shape=jax.ShapeDtypeStruct((b_n, a_n * D_W), BF16),
        compiler_params=_cparams("parallel"),
        name="fft_stage2",
    )(v4, w2, cc, sc)
    return y.reshape(n, D_W)


def _router_body(h_ref, w_ref, b_ref, idx_ref, wgt_ref, rank_ref, cnt_ref, carry_ref):
    i = pl.program_id(0)
    tm = h_ref.shape[0]

    @pl.when(i == 0)
    def _():
        carry_ref[...] = jnp.zeros_like(carry_ref)

    logits = jnp.dot(h_ref[...], w_ref[...], precision=_HI, preferred_element_type=F32) + b_ref[...]
    e_iota = lax.broadcasted_iota(jnp.int32, (tm, N_EXPERTS), 1).astype(F32)
    k_iota = lax.broadcasted_iota(jnp.int32, (tm, TOP_K), 1)
    work = logits
    mask = jnp.zeros((tm, N_EXPERTS), F32)
    idx4 = jnp.zeros((tm, TOP_K), F32)
    val4 = jnp.zeros((tm, TOP_K), F32)
    sels = []
    for k in range(TOP_K):
        mk = jnp.max(work, axis=-1, keepdims=True)
        ik = jnp.min(jnp.where(work == mk, e_iota, float(N_EXPERTS)), axis=-1, keepdims=True)
        sel = e_iota == ik
        sels.append(sel)
        mask = mask + sel.astype(F32)
        idx4 = jnp.where(k_iota == k, ik, idx4)
        val4 = jnp.where(k_iota == k, mk, val4)
        work = jnp.where(sel, -jnp.inf, work)
    ex = jnp.exp(val4 - jnp.max(val4, axis=-1, keepdims=True))
    wgt_ref[...] = ex / jnp.sum(ex, axis=-1, keepdims=True)
    idx_ref[...] = idx4.astype(jnp.int32)
    r = lax.broadcasted_iota(jnp.int32, (tm, tm), 0)
    c = lax.broadcasted_iota(jnp.int32, (tm, tm), 1)
    ltri = jnp.where(r > c, 1.0, 0.0).astype(BF16)
    rank = jnp.dot(ltri, mask.astype(BF16), preferred_element_type=F32) + carry_ref[...]
    rank4 = jnp.zeros((tm, TOP_K), F32)
    for k in range(TOP_K):
        rk = jnp.sum(jnp.where(sels[k], rank, 0.0), axis=-1, keepdims=True)
        rank4 = jnp.where(k_iota == k, rk, rank4)
    rank_ref[...] = rank4.astype(jnp.int32)
    carry_ref[...] = carry_ref[...] + jnp.sum(mask, axis=0, keepdims=True)
    cnt_ref[...] = carry_ref[...]


def router(hf, w_router, b_router, tm=ROW_TILE):
    t, d = hf.shape
    return pl.pallas_call(
        _router_body,
        grid=(t // tm,),
        in_specs=[
            pl.BlockSpec((tm, d), lambda i: (i, 0)),
            pl.BlockSpec((d, N_EXPERTS), lambda i: (0, 0)),
            pl.BlockSpec((1, N_EXPERTS), lambda i: (0, 0)),
        ],
        out_specs=[
            pl.BlockSpec((tm, TOP_K), lambda i: (i, 0)),
            pl.BlockSpec((tm, TOP_K), lambda i: (i, 0)),
            pl.BlockSpec((tm, TOP_K), lambda i: (i, 0)),
            pl.BlockSpec((1, N_EXPERTS), lambda i: (0, 0)),
        ],
        out_shape=[
            jax.ShapeDtypeStruct((t, TOP_K), jnp.int32),
            jax.ShapeDtypeStruct((t, TOP_K), F32),
            jax.ShapeDtypeStruct((t, TOP_K), jnp.int32),
            jax.ShapeDtypeStruct((1, N_EXPERTS), F32),
        ],
        scratch_shapes=[pltpu.VMEM((1, N_EXPERTS), F32)],
        compiler_params=_cparams("arbitrary"),
        name="router",
    )(hf, w_router, b_router[None])


def _row_gather_copy(src_hbm, row, dst_buf, dst_row, sem):
    return pltpu.make_async_copy(src_hbm.at[pl.ds(row, 1)], dst_buf.at[pl.ds(dst_row, 1)], sem)


def _expert_up_body(te_ref, nu_ref, tok_cur, tok_nxt, h_hbm, w_ref, b_ref, o_ref, xbuf, wbf, sems):
    i = pl.program_id(0)
    n_used = nu_ref[0]
    tm = xbuf.shape[1]

    def issue(tok_ref, slot):
        def body(r, _):
            _row_gather_copy(h_hbm, tok_ref[0, 0, r], xbuf.at[slot], r, sems.at[slot]).start()
            return 0
        lax.fori_loop(0, tm, body, 0)

    @pl.when(i == 0)
    def _():
        issue(tok_cur, 0)

    @pl.when(i + 1 < n_used)
    def _():
        issue(tok_nxt, (i + 1) % 2)

    @pl.when(i < n_used)
    def _():
        slot = i % 2
        pltpu.make_async_copy(h_hbm.at[pl.ds(0, tm)], xbuf.at[slot], sems.at[slot]).wait()

        @pl.when((i == 0) | (te_ref[i] != te_ref[jnp.maximum(i - 1, 0)]))
        def _():
            wbf[...] = w_ref[0].astype(BF16)

        x = xbuf[slot].astype(BF16)
        gu = jnp.dot(x, wbf[...], preferred_element_type=F32) + b_ref[0]
        gate = jnp.minimum(gu[:, :D_EXPERT], SWIGLU_LIMIT)
        up = jnp.clip(gu[:, D_EXPERT:], -SWIGLU_LIMIT, SWIGLU_LIMIT)
        act = (up + 1.0) * (gate * jax.nn.sigmoid(SWIGLU_ALPHA * gate))
        o_ref[...] = act.astype(o_ref.dtype)

    @pl.when(i >= n_used)
    def _():
        o_ref[...] = jnp.zeros_like(o_ref)


def expert_up(hf, tok_tiles, tile_expert, n_used, w_gu, b_gu, tm=MOE_TM):
    nt = tok_tiles.shape[0]
    d = hf.shape[1]
    f2 = w_gu.shape[2]
    grid_spec = pltpu.PrefetchScalarGridSpec(
        num_scalar_prefetch=2,
        grid=(nt,),
        in_specs=[
            pl.BlockSpec((1, 1, tm), lambda i, te, nu: (i, 0, 0), memory_space=pltpu.SMEM),
            pl.BlockSpec((1, 1, tm), lambda i, te, nu: (jnp.minimum(i + 1, nt - 1), 0, 0), memory_space=pltpu.SMEM),
            pl.BlockSpec(memory_space=pl.ANY),
            pl.BlockSpec((1, d, f2), lambda i, te, nu: (te[i], 0, 0), pipeline_mode=pl.Buffered(1)),
            pl.BlockSpec((1, 1, f2), lambda i, te, nu: (te[i], 0, 0)),
        ],
        out_specs=pl.BlockSpec((tm, f2 // 2), lambda i, te, nu: (i, 0)),
        scratch_shapes=[
            pltpu.VMEM((2, tm, d), F32),
            pltpu.VMEM((d, f2), BF16),
            pltpu.SemaphoreType.DMA((2,)),
        ],
    )
    return pl.pallas_call(
        _expert_up_body,
        grid_spec=grid_spec,
        out_shape=jax.ShapeDtypeStruct((nt * tm, f2 // 2), BF16),
        compiler_params=_cparams("arbitrary"),
        name="expert_up",
    )(tile_expert, n_used, tok_tiles, tok_tiles, hf, w_gu, b_gu[:, None, :])


def _expert_down_body(te_ref, nu_ref, a_ref, w_ref, b_ref, o_ref, wbf):
    i = pl.program_id(0)
    n_used = nu_ref[0]

    @pl.when(i < n_used)
    def _():
        @pl.when((i == 0) | (te_ref[i] != te_ref[jnp.maximum(i - 1, 0)]))
        def _():
            wbf[...] = w_ref[0].astype(BF16)

        o_ref[...] = jnp.dot(a_ref[...], wbf[...], preferred_element_type=F32) + b_ref[0]

    @pl.when(i >= n_used)
    def _():
        o_ref[...] = jnp.zeros_like(o_ref)


def expert_down(act, tile_expert, n_used, w_down, b_down, tm=MOE_TM):
    p, f = act.shape
    nt = p // tm
    d = w_down.shape[2]
    grid_spec = pltpu.PrefetchScalarGridSpec(
        num_scalar_prefetch=2,
        grid=(nt,),
        in_specs=[
            pl.BlockSpec((tm, f), lambda i, te, nu: (i, 0)),
            pl.BlockSpec((1, f, d), lambda i, te, nu: (te[i], 0, 0)),
            pl.BlockSpec((1, 1, d), lambda i, te, nu: (te[i], 0, 0)),
        ],
        out_specs=pl.BlockSpec((tm, d), lambda i, te, nu: (i, 0)),
        scratch_shapes=[pltpu.VMEM((f, d), BF16)],
    )
    return pl.pallas_call(
        _expert_down_body,
        grid_spec=grid_spec,
        out_shape=jax.ShapeDtypeStruct((p, d), F32),
        compiler_params=_cparams("arbitrary"),
        name="expert_down",
    )(tile_expert, n_used, act, w_down, b_down[:, None, :])


def _combine_body(slot_cur, slot_nxt, wgt_ref, x_ref, gate_ref, y_hbm, o_ref, ybuf, sems):
    i = pl.program_id(0)
    n_steps = pl.num_programs(0)
    tb = x_ref.shape[0]

    def issue(slot_ref, buf):
        for k in range(TOP_K):
            def body(r, _):
                _row_gather_copy(y_hbm, slot_ref[0, 0, k * tb + r], ybuf.at[buf, k], r, sems.at[buf]).start()
                return 0
            lax.fori_loop(0, tb, body, 0)

    @pl.when(i == 0)
    def _():
        issue(slot_cur, 0)

    @pl.when(i + 1 < n_steps)
    def _():
        issue(slot_nxt, (i + 1) % 2)

    buf = i % 2
    for k in range(TOP_K):
        pltpu.make_async_copy(y_hbm.at[pl.ds(0, tb)], ybuf.at[buf, k], sems.at[buf]).wait()
    w = wgt_ref[...]
    f = w[:, 0:1] * ybuf[buf, 0]
    for k in range(1, TOP_K):
        f = f + w[:, k:k + 1] * ybuf[buf, k]
    o_ref[...] = x_ref[...] + gate_ref[0] * f


def combine(x_all, ys, slot_tiles, wgt, gate_tbl, split_tile, tb=COMBINE_TB):
    t, d = x_all.shape
    nsteps = t // tb
    if split_tile is None:
        sel = lambda i: (0, 0, 0)
    else:
        sel = lambda i: (jnp.where(i >= split_tile, 1, 0), 0, 0)
    return pl.pallas_call(
        _combine_body,
        grid=(nsteps,),
        in_specs=[
            pl.BlockSpec((1, 1, TOP_K * tb), lambda i: (i, 0, 0), memory_space=pltpu.SMEM),
            pl.BlockSpec((1, 1, TOP_K * tb), lambda i: (jnp.minimum(i + 1, nsteps - 1), 0, 0),
                         memory_space=pltpu.SMEM),
            pl.BlockSpec((tb, TOP_K), lambda i: (i, 0)),
            pl.BlockSpec((tb, d), lambda i: (i, 0)),
            pl.BlockSpec((1, 1, d), sel),
            pl.BlockSpec(memory_space=pl.ANY),
        ],
        out_specs=pl.BlockSpec((tb, d), lambda i: (i, 0)),
        out_shape=jax.ShapeDtypeStruct((t, d), F32),
        scratch_shapes=[pltpu.VMEM((2, TOP_K, tb, d), F32), pltpu.SemaphoreType.DMA((2,))],
        compiler_params=_cparams("arbitrary"),
        name="moe_combine",
    )(slot_tiles, slot_tiles, wgt, x_all, gate_tbl, ys)


def moe_block(x_all, g_ffn, scale_tbl, shift_tbl, gate_tbl, split_rows, w_router, b_router, w_gu, b_gu,
              w_down, b_down):
    t, d = x_all.shape
    tm = MOE_TM
    hf = norm_mod(x_all, g_ffn, scale_tbl, shift_tbl, out_dtype=F32,
                  split_tile=None if split_rows is None else split_rows // ROW_TILE, name="norm_ffn")
    idx4, wgt4, rank4, counts = router(hf, w_router, b_router)
    nt = (t * TOP_K + N_EXPERTS * (tm - 1) + tm - 1) // tm
    cnt = counts[0].astype(jnp.int32)
    tiles_e = (cnt + tm - 1) // tm
    tile_end = jnp.cumsum(tiles_e)
    offsets = (tile_end - tiles_e) * tm
    n_used = tile_end[-1]
    slot4 = offsets[idx4] + rank4
    te = jnp.searchsorted(tile_end, jnp.arange(nt, dtype=jnp.int32), side="right").astype(jnp.int32)
    te = jnp.minimum(te, te[jnp.maximum(n_used - 1, 0)])
    tok_ids = jnp.broadcast_to(jnp.arange(t, dtype=jnp.int32)[:, None], (t, TOP_K))
    sorted_tok = jnp.zeros((nt * tm,), jnp.int32).at[slot4.reshape(-1)].set(tok_ids.reshape(-1))
    n_used_arr = n_used.astype(jnp.int32).reshape(1)
    act = expert_up(hf, sorted_tok.reshape(nt, 1, tm), te, n_used_arr, w_gu, b_gu)
    ys = expert_down(act, te, n_used_arr, w_down, b_down)
    tb = COMBINE_TB
    slot_tiles = slot4.reshape(t // tb, tb, TOP_K).transpose(0, 2, 1).reshape(t // tb, 1, TOP_K * tb)
    return combine(x_all, ys, slot_tiles, wgt4, gate_tbl,
                   None if split_rows is None else split_rows // tb)


def _rope_angles(n_tokens, rot_dim):
    rows = n_tokens // GRID_W
    row = jnp.repeat(jnp.arange(rows, dtype=F32), GRID_W)
    col = jnp.tile(jnp.arange(GRID_W, dtype=F32), rows)
    n_freq = rot_dim // 4
    inv_freq = ROPE_THETA ** (-jnp.arange(n_freq, dtype=F32) / n_freq)
    ang = jnp.concatenate([row[:, None] * inv_freq, col[:, None] * inv_freq], axis=-1)
    return jnp.cos(ang), jnp.sin(ang)


def _rope_tables(n, n_ctx):
    cos, sin = _rope_angles(n, 64)
    z = jnp.zeros_like(cos)
    pair = (jnp.concatenate([cos] * 4, axis=1), jnp.concatenate([-sin, sin, -sin, sin], axis=1))
    single = (jnp.concatenate([cos, cos, z, z], axis=1), jnp.concatenate([-sin, sin, z, z], axis=1))
    ident = (jnp.ones((n_ctx, LANES), F32), jnp.zeros((n_ctx, LANES), F32))
    return pair, single, ident


def _pad_heads(w, n_heads, widths, pad_to):
    k = w.shape[0]
    per = sum(widths)
    w3 = w.reshape(k, n_heads, per)[:, :, :widths[0]]
    return jnp.pad(w3, ((0, 0), (0, 0), (0, pad_to - widths[0]))).reshape(k, n_heads * pad_to)


def _kv_weight(w_ukv_l):
    kin = C_KV_RANK + 2 * C_ROPE
    w_kv3 = w_ukv_l.reshape(C_KV_RANK, C_HEADS, C_NOPE + C_HEAD_V)
    w_k = jnp.pad(w_kv3[:, :, :C_NOPE], ((0, 0), (0, 0), (0, C_QK_PAD - C_NOPE)))
    rope_rows = jnp.pad(jnp.eye(C_ROPE, dtype=F32), ((0, 0), (C_NOPE, C_QK_PAD - C_NOPE - C_ROPE)))
    rope_rows = jnp.broadcast_to(rope_rows[:, None, :], (C_ROPE, C_HEADS, C_QK_PAD))
    w_k = jnp.concatenate([w_k, rope_rows, jnp.zeros((C_ROPE, C_HEADS, C_QK_PAD), F32)], axis=0)
    w_v = jnp.pad(w_kv3[:, :, C_NOPE:], ((0, 2 * C_ROPE), (0, 0), (0, 0)))
    return jnp.concatenate([w_k.reshape(kin, -1), w_v.reshape(kin, -1)], axis=1).astype(BF16)


def _tbl(lat_row, ctx_row=None):
    rows = [lat_row] if ctx_row is None else [lat_row, ctx_row]
    return jnp.stack(rows)[:, None, :]


def kernel(x, c, ctx, c_ctx, w_mod, b_mod, g_mix, g_ffn, w_in_ab, ws_a, bs_a, sink_b, w_out_ab, w_in_cd, g_cq,
           w_uq, g_ckv, w_ukv, w_out_cd, w_router, b_router, w_gu, b_gu, w_down, b_down, g_final):
    bsz, n, d = x.shape
    n_ctx = ctx.shape[1]
    assert bsz == 1 and d == D_MODEL and n == FFT_A * FFT_B
    xl = x[0]
    xc = ctx[0]
    c_rows = jnp.zeros((8, d), F32).at[0].set(c[0]).at[1].set(c_ctx)
    pair_t, single_t, ident_t = _rope_tables(n, n_ctx)
    zeros_tbl = jnp.zeros((1, 1, d), F32)

    def mod_rows(l):
        mod = modulation(c_rows, w_mod[l], b_mod[l])
        lat = [mod[0, k * d:(k + 1) * d] for k in range(N_MOD)]
        cx = [mod[1, k * d:(k + 1) * d] for k in range(N_MOD)]
        return lat, cx

    lat, cx = mod_rows(0)
    h = norm_mod(xl, g_mix[0], _tbl(lat[1]), _tbl(lat[0]), out_dtype=BF16, name="norm_mix0")
    hc = norm_mod(xc, g_mix[0], _tbl(cx[1]), _tbl(cx[0]), out_dtype=BF16, name="norm_mix0_ctx")
    w_in = w_in_ab[0].astype(BF16)
    z = matmul([h], [w_in], out_dtype=BF16, tm=1024, tn=512, a_resident=True, name="proj_ab")
    zc = matmul([hc], [w_in], out_dtype=BF16, tm=n_ctx, tn=512, a_resident=True, name="proj_ab_ctx")
    ws_bf = ws_a[0].astype(BF16)
    bs_col = bs_a[0][:, :, None]
    a_lat = gmlp(z, ws_bf, bs_col)
    a_ctx = gmlp(zc, ws_bf, bs_col)
    q_lat = rope_q(z, *pair_t)
    q_ctx = rope_q(zc, *ident_t)
    kd, vd = rope_kv(z, *pair_t)
    kd_c, vd_c = rope_kv(zc, *ident_t)
    sink4 = sink_b[0].astype(F32).reshape(B_KV_HEADS, B_GROUP, 1, 1)
    b_lat = window_attn(q_lat, kd, vd, kd_c, vd_c, sink4, True)
    b_ctx = window_attn(q_ctx, None, None, kd_c, vd_c, sink4, False)
    w_out = w_out_ab[0].astype(BF16)
    xl = matmul([a_lat, b_lat], [w_out[:A_W], w_out[A_W:]], out_dtype=F32, tm=512, tn=1024,
                res=xl, gate=lat[2][None], name="out_ab")
    xc = matmul([a_ctx, b_ctx], [w_out[:A_W], w_out[A_W:]], out_dtype=F32, tm=n_ctx, tn=1024,
                res=xc, gate=cx[2][None], name="out_ab_ctx")
    x_all = jnp.concatenate([xl, xc], axis=0)
    x_all = moe_block(x_all, g_ffn[0], _tbl(lat[4], cx[4]), _tbl(lat[3], cx[3]), _tbl(lat[5], cx[5]), n,
                      w_router[0], b_router[0], w_gu[0], b_gu[0], w_down[0], b_down[0])
    xl = x_all[:n]
    xc = x_all[n:]

    lat, cx = mod_rows(1)
    h = norm_mod(xl, g_mix[1], _tbl(lat[1]), _tbl(lat[0]), out_dtype=BF16, name="norm_mix1")
    hc = norm_mod(xc, g_mix[1], _tbl(cx[1]), _tbl(cx[0]), out_dtype=BF16, name="norm_mix1_ctx")
    w_in = w_in_cd[0]
    w_cq = w_in[:, :CD_KV0].astype(BF16)
    w_kvr = jnp.pad(w_in[:, CD_KV0:CD_D0], ((0, 0), (0, 64))).astype(BF16)
    w_dz = w_in[:, CD_D0:].astype(BF16)
    cq = matmul([h], [w_cq], out_dtype=BF16, tm=1024, tn=C_Q_RANK, name="proj_cq")
    ckvkr = matmul([h], [w_kvr], out_dtype=BF16, tm=1024, tn=640, name="proj_ckv")
    ckvkr_c = matmul([hc], [w_kvr], out_dtype=BF16, tm=n_ctx, tn=640, name="proj_ckv_ctx")
    dz = matmul([h], [w_dz], out_dtype=F32, tm=1024, tn=D_W, name="proj_dz")
    cqn = norm_mod(cq, g_cq[0], jnp.zeros((1, 1, C_Q_RANK), F32), jnp.zeros((1, 1, C_Q_RANK), F32),
                   out_dtype=BF16, name="norm_cq")
    w_uq_p = _pad_heads(w_uq[0], C_HEADS, (C_NOPE + C_ROPE,), C_QK_PAD).astype(BF16)
    qp = matmul([cqn], [w_uq_p], out_dtype=BF16, tm=1024, tn=1024, name="proj_uq")
    kv_in = kvprep(ckvkr, g_ckv[0], *single_t)
    kv_in_c = kvprep(ckvkr_c, g_ckv[0], *ident_t)
    w_kv = _kv_weight(w_ukv[0])
    kv = matmul([kv_in], [w_kv], out_dtype=BF16, tm=1024, tn=1024, name="proj_ukv")
    kv_c = matmul([kv_in_c], [w_kv], out_dtype=BF16, tm=n_ctx, tn=1024, name="proj_ukv_ctx")
    c_lat = mla_attn(qp, kv, kv_c, *single_t)
    d_lat = fourier_mix(dz)
    w_out = w_out_cd[0].astype(BF16)
    n_c = C_HEADS * C_HEAD_V
    xl = matmul([c_lat, d_lat], [w_out[:n_c], w_out[n_c:]], out_dtype=F32, tm=512, tn=1024,
                res=xl, gate=lat[2][None], name="out_cd")
    xl = moe_block(xl, g_ffn[1], _tbl(lat[4]), _tbl(lat[3]), _tbl(lat[5]), None,
                   w_router[1], b_router[1], w_gu[1], b_gu[1], w_down[1], b_down[1])
    out = norm_mod(xl, g_final, zeros_tbl, zeros_tbl, out_dtype=F32, name="norm_final")
    return out[None]
```

```python
import functools
import math

import jax
import jax.numpy as jnp
from jax import lax
from jax.experimental import pallas as pl
from jax.experimental.pallas import tpu as pltpu

F32 = jnp.float32
BF16 = jnp.bfloat16

D_MODEL = 4096
GRID_W = 64
BLOCK = 128
CHUNK = 128
ROPE_THETA = 10000.0
EPS = 1e-6
NEG_INF = -1e30
N_MOD = 6
A_GROUPS = 8
A_GROUP_DIM = D_MODEL // 16
A_W = A_GROUPS * A_GROUP_DIM
B_HEAD_DIM = 64
B_HEADS = (D_MODEL // 2) // B_HEAD_DIM
B_KV_HEADS = B_HEADS // 8
B_GROUP = B_HEADS // B_KV_HEADS
B_QW = B_HEADS * B_HEAD_DIM
B_KW = B_KV_HEADS * B_HEAD_DIM
B_SCALE = B_HEAD_DIM ** -0.5
AB_KV0 = 2 * A_W + B_QW
C_HEAD_V = 128
C_HEADS = (3 * D_MODEL // 4) // C_HEAD_V
C_Q_RANK = 3 * D_MODEL // 16
C_KV_RANK = D_MODEL // 8
C_NOPE = 128
C_ROPE = 64
C_SCALE = (C_NOPE + C_ROPE) ** -0.5
C_QK_PAD = 256
D_GROUPS = 4
D_GROUP_DIM = D_MODEL // 16
D_W = D_GROUPS * D_GROUP_DIM
CD_KV0 = C_Q_RANK
CD_D0 = C_Q_RANK + C_KV_RANK + C_ROPE
N_EXPERTS = 32
TOP_K = 4
D_EXPERT = 3 * D_MODEL // 16
SWIGLU_LIMIT = 7.0
SWIGLU_ALPHA = 1.702
LOG2_E = 1.4426950408889634

V7X_VMEM_BYTES = 64 * 1024 * 1024
VMEM_LIMIT_BYTES = V7X_VMEM_BYTES - 8 * 1024 * 1024
LANES = 128

ROW_TILE = 256
MOE_TM = 256
COMBINE_TB = 128
UP_CHUNKS = 3
DOWN_CHUNKS = 4
MLA_TQ = 1024
MLA_TK = 1024
FFT_A = 64
FFT_B = 128


def _cparams(*sem):
    return pltpu.CompilerParams(dimension_semantics=sem, vmem_limit_bytes=VMEM_LIMIT_BYTES)


def _mm_body(n_a, has_bias, has_res, *refs):
    a_refs = refs[:n_a]
    w_refs = refs[n_a:2 * n_a]
    pos = 2 * n_a
    acc = None
    for a_ref, w_ref in zip(a_refs, w_refs):
        d = jnp.dot(a_ref[...], w_ref[...], preferred_element_type=F32)
        acc = d if acc is None else acc + d
    if has_bias:
        acc = acc + refs[pos][...]
        pos += 1
    if has_res:
        acc = refs[pos][...] + refs[pos + 1][...] * acc
    o_ref = refs[-1]
    o_ref[...] = acc.astype(o_ref.dtype)


def matmul(a_list, w_list, *, out_dtype, tm, tn, bias=None, res=None, gate=None, a_resident=False, name="mm"):
    m = a_list[0].shape[0]
    n = w_list[0].shape[1]
    assert m % tm == 0 and n % tn == 0, (m, tm, n, tn)
    if a_resident:
        grid = (m // tm, n // tn)
        im = lambda i, j: i
        jn = lambda i, j: j
    else:
        grid = (n // tn, m // tm)
        im = lambda j, i: i
        jn = lambda j, i: j
    in_specs = []
    for a in a_list:
        in_specs.append(pl.BlockSpec((tm, a.shape[1]), lambda *g: (im(*g), 0)))
    for w in w_list:
        in_specs.append(pl.BlockSpec((w.shape[0], tn), lambda *g: (0, jn(*g))))
    args = list(a_list) + list(w_list)
    if bias is not None:
        in_specs.append(pl.BlockSpec((1, tn), lambda *g: (0, jn(*g))))
        args.append(bias)
    if res is not None:
        in_specs.append(pl.BlockSpec((tm, tn), lambda *g: (im(*g), jn(*g))))
        in_specs.append(pl.BlockSpec((1, tn), lambda *g: (0, jn(*g))))
        args += [res, gate]
    return pl.pallas_call(
        functools.partial(_mm_body, len(a_list), bias is not None, res is not None),
        grid=grid,
        in_specs=in_specs,
        out_specs=pl.BlockSpec((tm, tn), lambda *g: (im(*g), jn(*g))),
        out_shape=jax.ShapeDtypeStruct((m, n), out_dtype),
        compiler_params=_cparams("parallel", "parallel"),
        name=name,
    )(*args)


def _mod_body(c_ref, w_ref, b_ref, o_ref):
    c = c_ref[...]
    a = (c * jax.nn.sigmoid(c)).astype(BF16)
    o_ref[...] = jnp.dot(a, w_ref[0].astype(BF16), preferred_element_type=F32) + b_ref[0]


def modulation(c_rows, w_mod, b_mod, layer, tn=1024):
    _, d, n = w_mod.shape
    return pl.pallas_call(
        _mod_body,
        grid=(n // tn,),
        in_specs=[
            pl.BlockSpec((8, d), lambda j: (0, 0)),
            pl.BlockSpec((1, d, tn), lambda j: (layer, 0, j)),
            pl.BlockSpec((1, 1, tn), lambda j: (layer, 0, j)),
        ],
        out_specs=pl.BlockSpec((8, tn), lambda j: (0, j)),
        out_shape=jax.ShapeDtypeStruct((8, n), F32),
        compiler_params=_cparams("parallel"),
        name="modulation",
    )(c_rows, w_mod, b_mod[:, None, :])


def _norm_mod_body(x_ref, g_ref, scale_ref, shift_ref, o_ref):
    x = x_ref[...].astype(F32)
    y = x * lax.rsqrt(jnp.mean(x * x, axis=-1, keepdims=True) + EPS)
    y = y * g_ref[...]
    o_ref[...] = (y * (1.0 + scale_ref[0]) + shift_ref[0]).astype(o_ref.dtype)


def norm_mod(x, g, scale_tbl, shift_tbl, *, out_dtype, rows=None, split_tile=None, tm=ROW_TILE, name="norm_mod"):
    d = x.shape[1]
    rows = x.shape[0] if rows is None else rows
    assert rows % tm == 0
    if split_tile is None:
        sel = lambda i: (0, 0, 0)
    else:
        sel = lambda i: (jnp.where(i >= split_tile, 1, 0), 0, 0)
    return pl.pallas_call(
        _norm_mod_body,
        grid=(rows // tm,),
        in_specs=[
            pl.BlockSpec((tm, d), lambda i: (i, 0)),
            pl.BlockSpec((1, d), lambda i: (0, 0)),
            pl.BlockSpec((1, 1, d), sel),
            pl.BlockSpec((1, 1, d), sel),
        ],
        out_specs=pl.BlockSpec((tm, d), lambda i: (i, 0)),
        out_shape=jax.ShapeDtypeStruct((rows, d), out_dtype),
        compiler_params=_cparams("parallel"),
        name=name,
    )(x, g[None].astype(F32), scale_tbl, shift_tbl)


def _gmlp_body(u_ref, v_ref, ws_ref, bs_ref, o_ref):
    for g in range(A_GROUPS):
        sl = slice(g * A_GROUP_DIM, (g + 1) * A_GROUP_DIM)
        ug = jax.nn.gelu(u_ref[:, sl].astype(F32))
        vf = jax.nn.gelu(v_ref[:, sl].astype(F32))
        mu = jnp.mean(vf, axis=-1, keepdims=True)
        vc = vf - mu
        var = jnp.mean(vc * vc, axis=-1, keepdims=True)
        vn = (vc * lax.rsqrt(var + EPS)).astype(BF16)
        mixed = jnp.dot(ws_ref[g], vn, preferred_element_type=F32) + bs_ref[g]
        o_ref[:, sl] = (ug * mixed).astype(o_ref.dtype)


def gmlp(z, ws_bf, bs_col):
    n = z.shape[0]
    return pl.pallas_call(
        _gmlp_body,
        grid=(n // CHUNK,),
        in_specs=[
            pl.BlockSpec((CHUNK, A_W), lambda i: (i, 0)),
            pl.BlockSpec((CHUNK, A_W), lambda i: (i, 1)),
            pl.BlockSpec((A_GROUPS, CHUNK, CHUNK), lambda i: (0, 0, 0)),
            pl.BlockSpec((A_GROUPS, CHUNK, 1), lambda i: (0, 0, 0)),
        ],
        out_specs=pl.BlockSpec((CHUNK, A_W), lambda i: (i, 0)),
        out_shape=jax.ShapeDtypeStruct((n, A_W), BF16),
        compiler_params=_cparams("parallel"),
        name="gmlp",
    )(z, z, ws_bf, bs_col)


def _swap_halves(x):
    lane = lax.broadcasted_iota(jnp.int32, x.shape, 1)
    return jnp.where((lane & 63) < 32, pltpu.roll(x, 96, 1), pltpu.roll(x, 32, 1))


def _rope_slab(x, cos, sin):
    return x * cos + _swap_halves(x) * sin


def _rope_q_body(z_ref, cos_ref, sin_ref, o_ref):
    cos = cos_ref[...]
    sin = sin_ref[...]
    for s in range(z_ref.shape[1] // LANES):
        sl = slice(s * LANES, (s + 1) * LANES)
        x = z_ref[:, sl].astype(F32)
        o_ref[:, sl] = (_rope_slab(x, cos, sin) * B_SCALE).astype(o_ref.dtype)


def rope_q(z, cos_t, sin_t, tm=ROW_TILE):
    n = z.shape[0]
    wb = 256
    col0 = (2 * A_W) // wb
    return pl.pallas_call(
        _rope_q_body,
        grid=(n // tm, B_QW // wb),
        in_specs=[
            pl.BlockSpec((tm, wb), lambda i, j: (i, col0 + j)),
            pl.BlockSpec((tm, LANES), lambda i, j: (i, 0)),
            pl.BlockSpec((tm, LANES), lambda i, j: (i, 0)),
        ],
        out_specs=pl.BlockSpec((tm, wb), lambda i, j: (i, j)),
        out_shape=jax.ShapeDtypeStruct((n, B_QW), BF16),
        compiler_params=_cparams("parallel", "parallel"),
        name="rope_q",
    )(z, cos_t, sin_t)


def _dup_heads(x):
    lane = lax.broadcasted_iota(jnp.int32, x.shape, 1)
    r = pltpu.roll(x, 64, 1)
    lo = lane < 64
    return jnp.where(lo, x, r), jnp.where(lo, r, x)


def _rope_kv_body(z_ref, cos_ref, sin_ref, kd_ref, vd_ref):
    cos = cos_ref[...]
    sin = sin_ref[...]
    for s in range(B_KW // LANES):
        xk = z_ref[:, s * LANES:(s + 1) * LANES].astype(F32)
        e, o = _dup_heads(_rope_slab(xk, cos, sin))
        kd_ref[:, (2 * s) * LANES:(2 * s + 1) * LANES] = e.astype(kd_ref.dtype)
        kd_ref[:, (2 * s + 1) * LANES:(2 * s + 2) * LANES] = o.astype(kd_ref.dtype)
        xv = z_ref[:, B_KW + s * LANES:B_KW + (s + 1) * LANES].astype(F32)
        e, o = _dup_heads(xv)
        vd_ref[:, (2 * s) * LANES:(2 * s + 1) * LANES] = e.astype(vd_ref.dtype)
        vd_ref[:, (2 * s + 1) * LANES:(2 * s + 2) * LANES] = o.astype(vd_ref.dtype)


def rope_kv(z, cos_t, sin_t, tm=ROW_TILE):
    n = z.shape[0]
    wb = 2 * B_KW
    col0 = AB_KV0 // wb
    out_w = B_KV_HEADS * LANES
    return pl.pallas_call(
        _rope_kv_body,
        grid=(n // tm,),
        in_specs=[
            pl.BlockSpec((tm, wb), lambda i: (i, col0)),
            pl.BlockSpec((tm, LANES), lambda i: (i, 0)),
            pl.BlockSpec((tm, LANES), lambda i: (i, 0)),
        ],
        out_specs=[pl.BlockSpec((tm, out_w), lambda i: (i, 0)), pl.BlockSpec((tm, out_w), lambda i: (i, 0))],
        out_shape=[jax.ShapeDtypeStruct((n, out_w), BF16), jax.ShapeDtypeStruct((n, out_w), BF16)],
        compiler_params=_cparams("parallel"),
        name="rope_kv",
    )(z, cos_t, sin_t)


def _window_attn_body(has_window, nb, *refs):
    if has_window:
        q_ref, kp_ref, kc_ref, kn_ref, vp_ref, vc_ref, vn_ref, kx_ref, vx_ref, sink_ref, o_ref = refs
    else:
        q_ref, kx_ref, vx_ref, sink_ref, o_ref = refs
    i = pl.program_id(0)
    n_ctx = kx_ref.shape[0]
    lane = lax.broadcasted_iota(jnp.int32, (BLOCK, LANES), 1)
    lo = lane < 64
    if has_window:
        r = lax.broadcasted_iota(jnp.int32, (BLOCK, BLOCK), 0)
        c = lax.broadcasted_iota(jnp.int32, (BLOCK, BLOCK), 1)
        ones = jnp.ones((BLOCK, BLOCK), jnp.bool_)
        valid = jnp.concatenate(
            [(c >= r) & (i > 0), ones, (c <= r) & (i < nb - 1), jnp.ones((BLOCK, n_ctx), jnp.bool_)], axis=1)
    for j in range(B_KV_HEADS):
        hs = slice(j * LANES, (j + 1) * LANES)
        if has_window:
            kj = jnp.concatenate([kp_ref[:, hs], kc_ref[:, hs], kn_ref[:, hs], kx_ref[:, hs]], axis=0)
            vj = jnp.concatenate([vp_ref[:, hs], vc_ref[:, hs], vn_ref[:, hs], vx_ref[:, hs]], axis=0)
        else:
            kj = kx_ref[:, hs]
            vj = vx_ref[:, hs]
        rows = []
        for m in range(B_GROUP // 2):
            slab = q_ref[:, (j * (B_GROUP // 2) + m) * LANES:(j * (B_GROUP // 2) + m + 1) * LANES]
            zero = jnp.zeros_like(slab)
            rows += [jnp.where(lo, slab, zero), jnp.where(lo, zero, slab)]
        qs = jnp.concatenate(rows, axis=0)
        s = lax.dot_general(qs, kj, (((1,), (1,)), ((), ())), preferred_element_type=F32)
        s = s.reshape(B_GROUP, BLOCK, kj.shape[0])
        if has_window:
            s = jnp.where(valid[None], s, NEG_INF)
        sink = sink_ref[j]
        mx = jnp.maximum(jnp.max(s, axis=-1, keepdims=True), sink)
        p = jnp.exp(s - mx)
        denom = jnp.sum(p, axis=-1, keepdims=True) + jnp.exp(sink - mx)
        o = jnp.dot(p.reshape(B_GROUP * BLOCK, kj.shape[0]).astype(BF16), vj, preferred_element_type=F32)
        o = o.reshape(B_GROUP, BLOCK, LANES) / denom
        for m in range(B_GROUP // 2):
            col = (j * (B_GROUP // 2) + m) * LANES
            o_ref[:, col:col + LANES] = jnp.where(lo, o[2 * m], o[2 * m + 1]).astype(o_ref.dtype)


def window_attn(q, kd, vd, kd_ctx, vd_ctx, sink4, has_window):
    n = q.shape[0]
    nb = n // BLOCK
    n_ctx = kd_ctx.shape[0]
    kw = B_KV_HEADS * LANES
    blk = lambda f: pl.BlockSpec((BLOCK, kw), f)
    full_ctx = pl.BlockSpec((n_ctx, kw), lambda i: (0, 0))
    in_specs = [pl.BlockSpec((BLOCK, B_QW), lambda i: (i, 0))]
    args = [q]
    if has_window:
        prev = lambda i: (jnp.maximum(i - 1, 0), 0)
        cur = lambda i: (i, 0)
        nxt = lambda i: (jnp.minimum(i + 1, nb - 1), 0)
        in_specs += [blk(prev), blk(cur), blk(nxt), blk(prev), blk(cur), blk(nxt)]
        args += [kd, kd, kd, vd, vd, vd]
    in_specs += [full_ctx, full_ctx, pl.BlockSpec((B_KV_HEADS, B_GROUP, 1, 1), lambda i: (0, 0, 0, 0))]
    args += [kd_ctx, vd_ctx, sink4]
    return pl.pallas_call(
        functools.partial(_window_attn_body, has_window, nb),
        grid=(nb,),
        in_specs=in_specs,
        out_specs=pl.BlockSpec((BLOCK, B_QW), lambda i: (i, 0)),
        out_shape=jax.ShapeDtypeStruct((n, B_QW), BF16),
        compiler_params=_cparams("parallel"),
        name="window_attn" if has_window else "ctx_attn",
    )(*args)


def _mla_rope(u, cos, sin):
    return u * cos + _swap_halves(u) * sin


def _kvprep_body(z_ref, g_ref, cos_ref, sin_ref, o_ref):
    ckv = z_ref[:, :C_KV_RANK].astype(F32)
    y = ckv * lax.rsqrt(jnp.mean(ckv * ckv, axis=-1, keepdims=True) + EPS) * g_ref[...]
    o_ref[:, :C_KV_RANK] = y.astype(o_ref.dtype)
    u = z_ref[:, C_KV_RANK:].astype(F32)
    o_ref[:, C_KV_RANK:] = _mla_rope(u, cos_ref[...], sin_ref[...]).astype(o_ref.dtype)


def kvprep(ckvkr, g_ckv, cos_t, sin_t, tm=ROW_TILE):
    n, w = ckvkr.shape
    return pl.pallas_call(
        _kvprep_body,
        grid=(n // tm,),
        in_specs=[
            pl.BlockSpec((tm, w), lambda i: (i, 0)),
            pl.BlockSpec((1, C_KV_RANK), lambda i: (0, 0)),
            pl.BlockSpec((tm, LANES), lambda i: (i, 0)),
            pl.BlockSpec((tm, LANES), lambda i: (i, 0)),
        ],
        out_specs=pl.BlockSpec((tm, w), lambda i: (i, 0)),
        out_shape=jax.ShapeDtypeStruct((n, w), BF16),
        compiler_params=_cparams("parallel"),
        name="kvprep",
    )(ckvkr, g_ckv[None].astype(F32), cos_t, sin_t)


def _mla_body(tk, q_ref, cos_ref, sin_ref, k_ref, v_ref, kx_ref, vx_ref, o_ref):
    tq = q_ref.shape[0]
    sc = C_SCALE * LOG2_E
    qn = q_ref[:, :LANES].astype(F32) * sc
    qr = _mla_rope(q_ref[:, LANES:].astype(F32), cos_ref[...], sin_ref[...]) * sc
    q = jnp.concatenate([qn, qr], axis=1).astype(BF16)

    def ones_column(rows):
        lane = lax.broadcasted_iota(jnp.int32, (rows, LANES), 1)
        return jnp.where(lane == 0, 1.0, 0.0).astype(BF16)

    def step(k, v, carry):
        m, acc = carry
        s = lax.dot_general(q, k, (((1,), (1,)), ((), ())), preferred_element_type=F32)
        m_new = jnp.maximum(m, jnp.max(s, axis=-1, keepdims=True))
        alpha = jnp.exp2(m - m_new)
        p = jnp.exp2(s - m_new).astype(BF16)
        v1 = jnp.concatenate([v, ones_column(v.shape[0])], axis=1)
        return m_new, alpha * acc + jnp.dot(p, v1, preferred_element_type=F32)

    carry = (jnp.full((tq, 1), -jnp.inf, F32), jnp.zeros((tq, C_HEAD_V + LANES), F32))
    for c in range(k_ref.shape[0] // tk):
        carry = step(k_ref[c * tk:(c + 1) * tk, :], v_ref[c * tk:(c + 1) * tk, :], carry)
    m, acc = step(kx_ref[...], vx_ref[...], carry)
    o_ref[...] = (acc[:, :C_HEAD_V] / acc[:, C_HEAD_V:C_HEAD_V + 1]).astype(o_ref.dtype)


def mla_attn(qp, kv, kv_ctx, cos_t, sin_t, tq=MLA_TQ, tk=MLA_TK):
    n = qp.shape[0]
    n_ctx = kv_ctx.shape[0]
    v_col0 = (C_HEADS * C_QK_PAD) // C_HEAD_V
    return pl.pallas_call(
        functools.partial(_mla_body, tk),
        grid=(C_HEADS, n // tq),
        in_specs=[
            pl.BlockSpec((tq, C_QK_PAD), lambda h, i: (i, h)),
            pl.BlockSpec((tq, LANES), lambda h, i: (i, 0)),
            pl.BlockSpec((tq, LANES), lambda h, i: (i, 0)),
            pl.BlockSpec((n, C_QK_PAD), lambda h, i: (0, h)),
            pl.BlockSpec((n, C_HEAD_V), lambda h, i: (0, v_col0 + h)),
            pl.BlockSpec((n_ctx, C_QK_PAD), lambda h, i: (0, h)),
            pl.BlockSpec((n_ctx, C_HEAD_V), lambda h, i: (0, v_col0 + h)),
        ],
        out_specs=pl.BlockSpec((tq, C_HEAD_V), lambda h, i: (i, h)),
        out_shape=jax.ShapeDtypeStruct((n, C_HEADS * C_HEAD_V), BF16),
        compiler_params=_cparams("parallel", "parallel"),
        name="mla_attn",
    )(qp, cos_t, sin_t, kv, kv, kv_ctx, kv_ctx)


_HI = lax.Precision.HIGHEST


def _fft1_body(x_ref, cs_ref, tr_ref, ti_ref, o_ref):
    u = jnp.dot(cs_ref[...], x_ref[...], precision=_HI, preferred_element_type=F32)
    ur, ui = u[:FFT_A], u[FFT_A:]
    reps = x_ref.shape[1] // LANES
    tr = jnp.concatenate([tr_ref[0]] * reps, axis=1)
    ti = jnp.concatenate([ti_ref[0]] * reps, axis=1)
    o_ref[0] = tr * ur - ti * ui
    o_ref[1] = tr * ui + ti * ur


def _fft2_body(v_ref, w2_ref, cc_ref, sc_ref, o_ref):
    vv = jnp.concatenate([v_ref[0, 0], v_ref[1, 0]], axis=0)
    g = jnp.dot(w2_ref[...], vv, precision=_HI, preferred_element_type=F32)
    gr, gi = g[:FFT_B], g[FFT_B:]
    for grp in range(D_GROUPS):
        sl = slice(grp * D_GROUP_DIM, (grp + 1) * D_GROUP_DIM)
        y = (jnp.dot(gr[:, sl], cc_ref[...], precision=_HI, preferred_element_type=F32)
             + jnp.dot(gi[:, sl], sc_ref[...], precision=_HI, preferred_element_type=F32))
        o_ref[:, sl] = y.astype(o_ref.dtype)


def _dft_tables(n):
    a_n, b_n = FFT_A, FFT_B
    assert a_n * b_n == n
    two_pi = 2.0 * math.pi

    def ang(i, j, period):
        return (two_pi / period) * ((i[:, None] * j[None, :]) % period).astype(F32)

    ia = jnp.arange(a_n, dtype=jnp.int32)
    ib = jnp.arange(b_n, dtype=jnp.int32)
    ic = jnp.arange(D_GROUP_DIM, dtype=jnp.int32)
    th_a = ang(ia, ia, a_n)
    cs = jnp.concatenate([jnp.cos(th_a), -jnp.sin(th_a)], axis=0)
    th_t = ang(ib, ia, n)
    tr = jnp.broadcast_to(jnp.cos(th_t)[:, :, None], (b_n, a_n, LANES))
    ti = jnp.broadcast_to(-jnp.sin(th_t)[:, :, None], (b_n, a_n, LANES))
    th_b = ang(ib, ib, b_n)
    cb, sb = jnp.cos(th_b), jnp.sin(th_b)
    w2 = jnp.concatenate([jnp.concatenate([cb, sb], axis=1), jnp.concatenate([-sb, cb], axis=1)], axis=0)
    th_c = ang(ic, ic, D_GROUP_DIM)
    norm = 1.0 / math.sqrt(n * D_GROUP_DIM)
    return cs, tr, ti, w2, jnp.cos(th_c) * norm, jnp.sin(th_c) * norm


def fourier_mix(dz):
    n = dz.shape[0]
    a_n, b_n = FFT_A, FFT_B
    cs, tr, ti, w2, cc, sc = _dft_tables(n)
    x2 = dz.reshape(a_n, b_n * D_W)
    v = pl.pallas_call(
        _fft1_body,
        grid=(b_n,),
        in_specs=[
            pl.BlockSpec((a_n, D_W), lambda b: (0, b)),
            pl.BlockSpec((2 * a_n, a_n), lambda b: (0, 0)),
            pl.BlockSpec((1, a_n, LANES), lambda b: (b, 0, 0)),
            pl.BlockSpec((1, a_n, LANES), lambda b: (b, 0, 0)),
        ],
        out_specs=pl.BlockSpec((2, a_n, D_W), lambda b: (0, 0, b)),
        out_shape=jax.ShapeDtypeStruct((2, a_n, b_n * D_W), F32),
        compiler_params=_cparams("parallel"),
        name="fft_stage1",
    )(x2, cs, tr, ti)
    v4 = v.reshape(2, a_n, b_n, D_W)
    y = pl.pallas_call(
        _fft2_body,
        grid=(a_n,),
        in_specs=[
            pl.BlockSpec((2, 1, b_n, D_W), lambda k: (0, k, 0, 0)),
            pl.BlockSpec((2 * b_n, 2 * b_n), lambda k: (0, 0)),
            pl.BlockSpec((D_GROUP_DIM, D_GROUP_DIM), lambda k: (0, 0)),
            pl.BlockSpec((D_GROUP_DIM, D_GROUP_DIM), lambda k: (0, 0)),
        ],
        out_specs=pl.BlockSpec((b_n, D_W), lambda k: (0, k)),
        out_shape=jax.ShapeDtypeStruct((b_n, a_n * D_W), BF16),
        compiler_params=_cparams("parallel"),
        name="fft_stage2",
    )(v4, w2, cc, sc)
    return y.reshape(n, D_W)


def _router_body(h_ref, w_ref, b_ref, idx_ref, wgt_ref, rank_ref, cnt_ref, carry_ref):
    i = pl.program_id(0)
    tm = h_ref.shape[0]

    @pl.when(i == 0)
    def _():
        carry_ref[...] = jnp.zeros_like(carry_ref)

    logits = jnp.dot(h_ref[...], w_ref[0], precision=_HI, preferred_element_type=F32) + b_ref[0]
    e_iota = lax.broadcasted_iota(jnp.int32, (tm, N_EXPERTS), 1).astype(F32)
    k_iota = lax.broadcasted_iota(jnp.int32, (tm, TOP_K), 1)
    work = logits
    mask = jnp.zeros((tm, N_EXPERTS), F32)
    idx4 = jnp.zeros((tm, TOP_K), F32)
    val4 = jnp.zeros((tm, TOP_K), F32)
    sels = []
    for k in range(TOP_K):
        mk = jnp.max(work, axis=-1, keepdims=True)
        ik = jnp.min(jnp.where(work == mk, e_iota, float(N_EXPERTS)), axis=-1, keepdims=True)
        sel = e_iota == ik
        sels.append(sel)
        mask = mask + sel.astype(F32)
        idx4 = jnp.where(k_iota == k, ik, idx4)
        val4 = jnp.where(k_iota == k, mk, val4)
        work = jnp.where(sel, -jnp.inf, work)
    ex = jnp.exp(val4 - jnp.max(val4, axis=-1, keepdims=True))
    wgt_ref[...] = ex / jnp.sum(ex, axis=-1, keepdims=True)
    idx_ref[...] = idx4.astype(jnp.int32)
    r = lax.broadcasted_iota(jnp.int32, (tm, tm), 0)
    c = lax.broadcasted_iota(jnp.int32, (tm, tm), 1)
    ltri = jnp.where(r > c, 1.0, 0.0).astype(BF16)
    rank = jnp.dot(ltri, mask.astype(BF16), preferred_element_type=F32) + carry_ref[...]
    rank4 = jnp.zeros((tm, TOP_K), F32)
    for k in range(TOP_K):
        rk = jnp.sum(jnp.where(sels[k], rank, 0.0), axis=-1, keepdims=True)
        rank4 = jnp.where(k_iota == k, rk, rank4)
    rank_ref[...] = rank4.astype(jnp.int32)
    carry_ref[...] = carry_ref[...] + jnp.sum(mask, axis=0, keepdims=True)
    cnt_ref[...] = carry_ref[...]


def router(hf, w_router, b_router, layer, tm=ROW_TILE):
    t, d = hf.shape
    return pl.pallas_call(
        _router_body,
        grid=(t // tm,),
        in_specs=[
            pl.BlockSpec((tm, d), lambda i: (i, 0)),
            pl.BlockSpec((1, d, N_EXPERTS), lambda i: (layer, 0, 0)),
            pl.BlockSpec((1, 1, N_EXPERTS), lambda i: (layer, 0, 0)),
        ],
        out_specs=[
            pl.BlockSpec((tm, TOP_K), lambda i: (i, 0)),
            pl.BlockSpec((tm, TOP_K), lambda i: (i, 0)),
            pl.BlockSpec((tm, TOP_K), lambda i: (i, 0)),
            pl.BlockSpec((1, N_EXPERTS), lambda i: (0, 0)),
        ],
        out_shape=[
            jax.ShapeDtypeStruct((t, TOP_K), jnp.int32),
            jax.ShapeDtypeStruct((t, TOP_K), F32),
            jax.ShapeDtypeStruct((t, TOP_K), jnp.int32),
            jax.ShapeDtypeStruct((1, N_EXPERTS), F32),
        ],
        scratch_shapes=[pltpu.VMEM((1, N_EXPERTS), F32)],
        compiler_params=_cparams("arbitrary"),
        name="router",
    )(hf, w_router, b_router[:, None, :])


def _row_copy(src, src_row, dst, dst_row, sem):
    return pltpu.make_async_copy(src.at[pl.ds(src_row, 1)], dst.at[pl.ds(dst_row, 1)], sem)


def _first_tile_of_expert(te_ref, i):
    return (i == 0) | (te_ref[i] != te_ref[jnp.maximum(i - 1, 0)])


def _expert_up_body(layer, te_ref, nx_ref, nu_ref, tok_cur, tok_nxt, h_hbm, w_hbm, b_ref, o_ref,
                    xbuf, wst, wbf, sems, wsem):
    i = pl.program_id(0)
    nt = pl.num_programs(0)
    n_used = nu_ref[0]
    tm = xbuf.shape[1]

    def gather_start(tok_ref, slot, r):
        _row_copy(h_hbm, tok_ref[0, 0, r], xbuf.at[slot], r, sems.at[slot]).start()

    def gather_wait(slot):
        pltpu.make_async_copy(h_hbm.at[pl.ds(0, tm)], xbuf.at[slot], sems.at[slot]).wait()

    def weight_copy(e):
        return pltpu.make_async_copy(w_hbm.at[layer, e], wst, wsem)

    @pl.when(i == 0)
    def _():
        weight_copy(te_ref[0]).start()

        def body(r, _):
            gather_start(tok_cur, 0, r)
            return 0
        lax.fori_loop(0, tm, body, 0)

    @pl.when(i < n_used)
    def _():
        slot = i % 2
        nxt = (i + 1) % 2
        gather_wait(slot)

        @pl.when(_first_tile_of_expert(te_ref, i))
        def _():
            weight_copy(te_ref[i]).wait()
            wbf[...] = wst[...].astype(BF16)

            @pl.when(nx_ref[i] >= 0)
            def _():
                weight_copy(nx_ref[i]).start()

        x = xbuf[slot].astype(BF16)
        f = D_EXPERT
        cw = f // UP_CHUNKS
        per = -(-tm // UP_CHUNKS)
        bias = b_ref[0, 0]
        for c in range(UP_CHUNKS):
            for r in range(c * per, min(tm, (c + 1) * per)):
                gather_start(tok_nxt, nxt, r)
            gs = slice(c * cw, (c + 1) * cw)
            us = slice(f + c * cw, f + (c + 1) * cw)
            g = jnp.dot(x, wbf[:, gs], preferred_element_type=F32) + bias[:, gs]
            u = jnp.dot(x, wbf[:, us], preferred_element_type=F32) + bias[:, us]
            gate = jnp.minimum(g, SWIGLU_LIMIT)
            up = jnp.clip(u, -SWIGLU_LIMIT, SWIGLU_LIMIT)
            o_ref[:, gs] = ((up + 1.0) * (gate * jax.nn.sigmoid(SWIGLU_ALPHA * gate))).astype(o_ref.dtype)

        @pl.when(i == nt - 1)
        def _():
            gather_wait(nxt)

    @pl.when(i >= n_used)
    def _():
        @pl.when(i == n_used)
        def _():
            gather_wait(i % 2)

        o_ref[...] = jnp.zeros_like(o_ref)


def expert_up(hf, tok_tiles, tile_expert, next_expert, n_used, w_gu, b_gu, layer, tm=MOE_TM):
    nt = tok_tiles.shape[0]
    d = hf.shape[1]
    f2 = w_gu.shape[3]
    grid_spec = pltpu.PrefetchScalarGridSpec(
        num_scalar_prefetch=3,
        grid=(nt,),
        in_specs=[
            pl.BlockSpec((1, 1, tm), lambda i, *_: (i, 0, 0), memory_space=pltpu.SMEM),
            pl.BlockSpec((1, 1, tm), lambda i, *_: (jnp.minimum(i + 1, nt - 1), 0, 0), memory_space=pltpu.SMEM),
            pl.BlockSpec(memory_space=pl.ANY),
            pl.BlockSpec(memory_space=pl.ANY),
            pl.BlockSpec((1, 1, 1, f2), lambda i, te, nx, nu: (layer, te[i], 0, 0)),
        ],
        out_specs=pl.BlockSpec((tm, f2 // 2), lambda i, *_: (i, 0)),
        scratch_shapes=[
            pltpu.VMEM((2, tm, d), F32),
            pltpu.VMEM((d, f2), F32),
            pltpu.VMEM((d, f2), BF16),
            pltpu.SemaphoreType.DMA((2,)),
            pltpu.SemaphoreType.DMA(()),
        ],
    )
    return pl.pallas_call(
        functools.partial(_expert_up_body, layer),
        grid_spec=grid_spec,
        out_shape=jax.ShapeDtypeStruct((nt * tm, f2 // 2), BF16),
        compiler_params=_cparams("arbitrary"),
        name="expert_up",
    )(tile_expert, next_expert, n_used, tok_tiles, tok_tiles, hf, w_gu, b_gu[:, :, None, :])


def _expert_down_body(te_ref, nu_ref, dst_prev, dst_cur, a_ref, w_ref, b_ref, y_hbm, ybuf, wbf, sems):
    i = pl.program_id(0)
    n_used = nu_ref[0]
    tm = a_ref.shape[0]
    d = ybuf.shape[2]

    def scatter_start(dst_ref, slot, r):
        _row_copy(ybuf.at[slot], r, y_hbm, dst_ref[0, 0, r], sems.at[slot]).start()

    def scatter_wait(slot):
        pltpu.make_async_copy(ybuf.at[slot], y_hbm.at[pl.ds(0, tm)], sems.at[slot]).wait()

    def compute(slot, prev, scatter_prev):
        a = a_ref[...]
        cw = d // DOWN_CHUNKS
        per = -(-tm // DOWN_CHUNKS)
        for c in range(DOWN_CHUNKS):
            if scatter_prev:
                for r in range(c * per, min(tm, (c + 1) * per)):
                    scatter_start(dst_prev, prev, r)
            cs = slice(c * cw, (c + 1) * cw)
            ybuf[slot, :, cs] = jnp.dot(a, wbf[:, cs], preferred_element_type=F32) + b_ref[0, 0][:, cs]

    @pl.when(i < n_used)
    def _():
        slot = i % 2
        prev = (i + 1) % 2

        @pl.when(i >= 2)
        def _():
            scatter_wait(slot)

        @pl.when(_first_tile_of_expert(te_ref, i))
        def _():
            wbf[...] = w_ref[0, 0].astype(BF16)

        @pl.when(i == 0)
        def _():
            compute(slot, prev, False)

        @pl.when(i > 0)
        def _():
            compute(slot, prev, True)

        @pl.when(i == n_used - 1)
        def _():
            def body(r, _):
                scatter_start(dst_cur, slot, r)
                return 0
            lax.fori_loop(0, tm, body, 0)

            @pl.when(i >= 1)
            def _():
                scatter_wait(prev)

            scatter_wait(slot)

    @pl.when(i >= n_used)
    def _():
        @pl.when(i == n_used)
        def _():
            ybuf[0] = jnp.zeros((tm, d), F32)

        zero_fill = pltpu.make_async_copy(ybuf.at[0], y_hbm.at[pl.ds(pl.multiple_of(i * tm, tm), tm)], sems.at[0])
        zero_fill.start()
        zero_fill.wait()


def expert_down(act, dst_tiles, tile_expert, n_used, w_down, b_down, layer, tm=MOE_TM):
    p, f = act.shape
    nt = p // tm
    d = w_down.shape[3]
    grid_spec = pltpu.PrefetchScalarGridSpec(
        num_scalar_prefetch=2,
        grid=(nt,),
        in_specs=[
            pl.BlockSpec((1, 1, tm), lambda i, te, nu: (jnp.maximum(i - 1, 0), 0, 0), memory_space=pltpu.SMEM),
            pl.BlockSpec((1, 1, tm), lambda i, te, nu: (i, 0, 0), memory_space=pltpu.SMEM),
            pl.BlockSpec((tm, f), lambda i, te, nu: (i, 0)),
            pl.BlockSpec((1, 1, f, d), lambda i, te, nu: (layer, te[i], 0, 0)),
            pl.BlockSpec((1, 1, 1, d), lambda i, te, nu: (layer, te[i], 0, 0)),
        ],
        out_specs=pl.BlockSpec(memory_space=pl.ANY),
        scratch_shapes=[pltpu.VMEM((2, tm, d), F32), pltpu.VMEM((f, d), BF16), pltpu.SemaphoreType.DMA((2,))],
    )
    return pl.pallas_call(
        _expert_down_body,
        grid_spec=grid_spec,
        out_shape=jax.ShapeDtypeStruct((p, d), F32),
        compiler_params=_cparams("arbitrary"),
        name="expert_down",
    )(tile_expert, n_used, dst_tiles, dst_tiles, act, w_down, b_down[:, :, None, :])


def _combine_body(wgt_ref, x_ref, gate_ref, y0_ref, y1_ref, y2_ref, y3_ref, o_ref):
    w = wgt_ref[...]
    f = w[:, 0:1] * y0_ref[...]
    for k, y_ref in enumerate((y1_ref, y2_ref, y3_ref), start=1):
        f = f + w[:, k:k + 1] * y_ref[...]
    o_ref[...] = x_ref[...] + gate_ref[0] * f


def combine(x_all, y_planes, wgt, gate_tbl, split_tile, tb=COMBINE_TB):
    t, d = x_all.shape
    nsteps = t // tb
    if split_tile is None:
        sel = lambda i: (0, 0, 0)
    else:
        sel = lambda i: (jnp.where(i >= split_tile, 1, 0), 0, 0)
    plane = lambda k: pl.BlockSpec((tb, d), lambda i: (k * nsteps + i, 0))
    return pl.pallas_call(
        _combine_body,
        grid=(nsteps,),
        in_specs=[
            pl.BlockSpec((tb, TOP_K), lambda i: (i, 0)),
            pl.BlockSpec((tb, d), lambda i: (i, 0)),
            pl.BlockSpec((1, 1, d), sel),
        ] + [plane(k) for k in range(TOP_K)],
        out_specs=pl.BlockSpec((tb, d), lambda i: (i, 0)),
        out_shape=jax.ShapeDtypeStruct((t, d), F32),
        compiler_params=_cparams("parallel"),
        name="moe_combine",
    )(wgt, x_all, gate_tbl, *([y_planes] * TOP_K))


def _routing_tables(idx4, rank4, counts, t, tm, nt):
    i32 = jnp.int32
    e_ids = jnp.arange(N_EXPERTS, dtype=i32)
    cnt = counts[0].astype(i32)
    tiles_e = (cnt + tm - 1) // tm
    tile_end = jnp.cumsum(tiles_e)
    offsets = (tile_end - tiles_e) * tm
    n_used = tile_end[-1]
    slot4 = jnp.sum(jnp.where(idx4[:, :, None] == e_ids, offsets, 0), axis=-1) + rank4
    te = jnp.sum((tile_end[None, :] <= jnp.arange(nt, dtype=i32)[:, None]).astype(i32), axis=1)
    last_e = jnp.sum(jnp.where(jnp.arange(nt, dtype=i32) == n_used - 1, te, 0))
    te = jnp.minimum(te, last_e)
    later_used = (tiles_e[None, :] > 0) & (e_ids[None, :] > e_ids[:, None])
    nxt_e = jnp.min(jnp.where(later_used, e_ids[None, :], N_EXPERTS), axis=1)
    nxt_e = jnp.where(nxt_e >= N_EXPERTS, -1, nxt_e)
    nx = jnp.sum(jnp.where(te[:, None] == e_ids, nxt_e, 0), axis=1).astype(i32)
    assign = jnp.arange(t * TOP_K, dtype=i32)
    code = jnp.full((nt * tm,), -1, i32).at[slot4.reshape(-1)].set(assign)
    is_pad = code < 0
    tok = jnp.where(is_pad, 0, code // TOP_K)
    pad_row = TOP_K * t + jnp.cumsum(is_pad.astype(i32)) - 1
    dst = jnp.where(is_pad, pad_row, (code % TOP_K) * t + code // TOP_K)
    return tok, dst, te.astype(i32), nx, n_used.astype(i32).reshape(1)


def moe_block(x_all, g_ffn, scale_tbl, shift_tbl, gate_tbl, split_rows, w_router, b_router, w_gu, b_gu,
              w_down, b_down, layer):
    t, d = x_all.shape
    tm = MOE_TM
    hf = norm_mod(x_all, g_ffn, scale_tbl, shift_tbl, out_dtype=F32,
                  split_tile=None if split_rows is None else split_rows // ROW_TILE, name="norm_ffn")
    idx4, wgt4, rank4, counts = router(hf, w_router, b_router, layer)
    nt = (t * TOP_K + N_EXPERTS * (tm - 1) + tm - 1) // tm
    tok, dst, te, nx, n_used = _routing_tables(idx4, rank4, counts, t, tm, nt)
    act = expert_up(hf, tok.reshape(nt, 1, tm), te, nx, n_used, w_gu, b_gu, layer)
    y_planes = expert_down(act, dst.reshape(nt, 1, tm), te, n_used, w_down, b_down, layer)
    return combine(x_all, y_planes, wgt4, gate_tbl, None if split_rows is None else split_rows // COMBINE_TB)


def _rope_angles(n_tokens, rot_dim):
    rows = n_tokens // GRID_W
    row = jnp.repeat(jnp.arange(rows, dtype=F32), GRID_W)
    col = jnp.tile(jnp.arange(GRID_W, dtype=F32), rows)
    n_freq = rot_dim // 4
    inv_freq = ROPE_THETA ** (-jnp.arange(n_freq, dtype=F32) / n_freq)
    ang = jnp.concatenate([row[:, None] * inv_freq, col[:, None] * inv_freq], axis=-1)
    return jnp.cos(ang), jnp.sin(ang)


def _rope_tables(n, n_ctx):
    cos, sin = _rope_angles(n, 64)
    z = jnp.zeros_like(cos)
    pair = (jnp.concatenate([cos] * 4, axis=1), jnp.concatenate([-sin, sin, -sin, sin], axis=1))
    single = (jnp.concatenate([cos, cos, z, z], axis=1), jnp.concatenate([-sin, sin, z, z], axis=1))
    ident = (jnp.ones((n_ctx, LANES), F32), jnp.zeros((n_ctx, LANES), F32))
    return pair, single, ident


def _pad_heads(w, n_heads, widths, pad_to):
    k = w.shape[0]
    per = sum(widths)
    w3 = w.reshape(k, n_heads, per)[:, :, :widths[0]]
    return jnp.pad(w3, ((0, 0), (0, 0), (0, pad_to - widths[0]))).reshape(k, n_heads * pad_to)


def _kv_weight(w_ukv_l):
    kin = C_KV_RANK + 2 * C_ROPE
    w_kv3 = w_ukv_l.reshape(C_KV_RANK, C_HEADS, C_NOPE + C_HEAD_V)
    w_k = jnp.pad(w_kv3[:, :, :C_NOPE], ((0, 0), (0, 0), (0, C_QK_PAD - C_NOPE)))
    rope_rows = jnp.pad(jnp.eye(C_ROPE, dtype=F32), ((0, 0), (C_NOPE, C_QK_PAD - C_NOPE - C_ROPE)))
    rope_rows = jnp.broadcast_to(rope_rows[:, None, :], (C_ROPE, C_HEADS, C_QK_PAD))
    w_k = jnp.concatenate([w_k, rope_rows, jnp.zeros((C_ROPE, C_HEADS, C_QK_PAD), F32)], axis=0)
    w_v = jnp.pad(w_kv3[:, :, C_NOPE:], ((0, 2 * C_ROPE), (0, 0), (0, 0)))
    return jnp.concatenate([w_k.reshape(kin, -1), w_v.reshape(kin, -1)], axis=1).astype(BF16)


def _tbl(lat_row, ctx_row=None):
    rows = [lat_row] if ctx_row is None else [lat_row, ctx_row]
    return jnp.stack(rows)[:, None, :]


def kernel(x, c, ctx, c_ctx, w_mod, b_mod, g_mix, g_ffn, w_in_ab, ws_a, bs_a, sink_b, w_out_ab, w_in_cd, g_cq,
           w_uq, g_ckv, w_ukv, w_out_cd, w_router, b_router, w_gu, b_gu, w_down, b_down, g_final):
    bsz, n, d = x.shape
    n_ctx = ctx.shape[1]
    assert bsz == 1 and d == D_MODEL and n == FFT_A * FFT_B
    xl = x[0]
    xc = ctx[0]
    c_rows = jnp.zeros((8, d), F32).at[0].set(c[0]).at[1].set(c_ctx)
    pair_t, single_t, ident_t = _rope_tables(n, n_ctx)
    zeros_tbl = jnp.zeros((1, 1, d), F32)

    def mod_rows(l):
        mod = modulation(c_rows, w_mod, b_mod, l)
        lat = [mod[0, k * d:(k + 1) * d] for k in range(N_MOD)]
        cx = [mod[1, k * d:(k + 1) * d] for k in range(N_MOD)]
        return lat, cx

    lat, cx = mod_rows(0)
    h = norm_mod(xl, g_mix[0], _tbl(lat[1]), _tbl(lat[0]), out_dtype=BF16, name="norm_mix0")
    hc = norm_mod(xc, g_mix[0], _tbl(cx[1]), _tbl(cx[0]), out_dtype=BF16, name="norm_mix0_ctx")
    w_in = w_in_ab[0].astype(BF16)
    z = matmul([h], [w_in], out_dtype=BF16, tm=1024, tn=512, a_resident=True, name="proj_ab")
    zc = matmul([hc], [w_in], out_dtype=BF16, tm=n_ctx, tn=512, a_resident=True, name="proj_ab_ctx")
    ws_bf = ws_a[0].astype(BF16)
    bs_col = bs_a[0][:, :, None]
    a_lat = gmlp(z, ws_bf, bs_col)
    a_ctx = gmlp(zc, ws_bf, bs_col)
    q_lat = rope_q(z, *pair_t)
    q_ctx = rope_q(zc, *ident_t)
    kd, vd = rope_kv(z, *pair_t)
    kd_c, vd_c = rope_kv(zc, *ident_t)
    sink4 = sink_b[0].astype(F32).reshape(B_KV_HEADS, B_GROUP, 1, 1)
    b_lat = window_attn(q_lat, kd, vd, kd_c, vd_c, sink4, True)
    b_ctx = window_attn(q_ctx, None, None, kd_c, vd_c, sink4, False)
    w_out = w_out_ab[0].astype(BF16)
    xl = matmul([a_lat, b_lat], [w_out[:A_W], w_out[A_W:]], out_dtype=F32, tm=512, tn=1024,
                res=xl, gate=lat[2][None], name="out_ab")
    xc = matmul([a_ctx, b_ctx], [w_out[:A_W], w_out[A_W:]], out_dtype=F32, tm=n_ctx, tn=1024,
                res=xc, gate=cx[2][None], name="out_ab_ctx")
    x_all = jnp.concatenate([xl, xc], axis=0)
    x_all = moe_block(x_all, g_ffn[0], _tbl(lat[4], cx[4]), _tbl(lat[3], cx[3]), _tbl(lat[5], cx[5]), n,
                      w_router, b_router, w_gu, b_gu, w_down, b_down, 0)
    xl = x_all[:n]
    xc = x_all[n:]

    lat, cx = mod_rows(1)
    h = norm_mod(xl, g_mix[1], _tbl(lat[1]), _tbl(lat[0]), out_dtype=BF16, name="norm_mix1")
    hc = norm_mod(xc, g_mix[1], _tbl(cx[1]), _tbl(cx[0]), out_dtype=BF16, name="norm_mix1_ctx")
    w_in = w_in_cd[0]
    w_cq = w_in[:, :CD_KV0].astype(BF16)
    w_kvr = jnp.pad(w_in[:, CD_KV0:CD_D0], ((0, 0), (0, 64))).astype(BF16)
    w_dz = w_in[:, CD_D0:].astype(BF16)
    cq = matmul([h], [w_cq], out_dtype=BF16, tm=1024, tn=C_Q_RANK, name="proj_cq")
    ckvkr = matmul([h], [w_kvr], out_dtype=BF16, tm=1024, tn=640, name="proj_ckv")
    ckvkr_c = matmul([hc], [w_kvr], out_dtype=BF16, tm=n_ctx, tn=640, name="proj_ckv_ctx")
    dz = matmul([h], [w_dz], out_dtype=F32, tm=1024, tn=D_W, name="proj_dz")
    cqn = norm_mod(cq, g_cq[0], jnp.zeros((1, 1, C_Q_RANK), F32), jnp.zeros((1, 1, C_Q_RANK), F32),
                   out_dtype=BF16, name="norm_cq")
    w_uq_p = _pad_heads(w_uq[0], C_HEADS, (C_NOPE + C_ROPE,), C_QK_PAD).astype(BF16)
    qp = matmul([cqn], [w_uq_p], out_dtype=BF16, tm=1024, tn=1024, name="proj_uq")
    kv_in = kvprep(ckvkr, g_ckv[0], *single_t)
    kv_in_c = kvprep(ckvkr_c, g_ckv[0], *ident_t)
    w_kv = _kv_weight(w_ukv[0])
    kv = matmul([kv_in], [w_kv], out_dtype=BF16, tm=1024, tn=1024, name="proj_ukv")
    kv_c = matmul([kv_in_c], [w_kv], out_dtype=BF16, tm=n_ctx, tn=1024, name="proj_ukv_ctx")
    c_lat = mla_attn(qp, kv, kv_c, *single_t)
    d_lat = fourier_mix(dz)
    w_out = w_out_cd[0].astype(BF16)
    n_c = C_HEADS * C_HEAD_V
    xl = matmul([c_lat, d_lat], [w_out[:n_c], w_out[n_c:]], out_dtype=F32, tm=512, tn=1024,
                res=xl, gate=lat[2][None], name="out_cd")
    xl = moe_block(xl, g_ffn[1], _tbl(lat[4]), _tbl(lat[3]), _tbl(lat[5]), None,
                   w_router, b_router, w_gu, b_gu, w_down, b_down, 1)
    out = norm_mod(xl, g_final, zeros_tbl, zeros_tbl, out_dtype=F32, name="norm_final")
    return out[None]
```

```python
import functools
import math

import jax
import jax.numpy as jnp
from jax import lax
from jax.experimental import pallas as pl
from jax.experimental.pallas import tpu as pltpu

F32 = jnp.float32
BF16 = jnp.bfloat16

D_MODEL = 4096
GRID_W = 64
BLOCK = 128
CHUNK = 128
ROPE_THETA = 10000.0
EPS = 1e-6
NEG_INF = -1e30
N_MOD = 6
A_GROUPS = 8
A_GROUP_DIM = D_MODEL // 16
A_W = A_GROUPS * A_GROUP_DIM
B_HEAD_DIM = 64
B_HEADS = (D_MODEL // 2) // B_HEAD_DIM
B_KV_HEADS = B_HEADS // 8
B_GROUP = B_HEADS // B_KV_HEADS
B_QW = B_HEADS * B_HEAD_DIM
B_KW = B_KV_HEADS * B_HEAD_DIM
B_SCALE = B_HEAD_DIM ** -0.5
AB_KV0 = 2 * A_W + B_QW
C_HEAD_V = 128
C_HEADS = (3 * D_MODEL // 4) // C_HEAD_V
C_Q_RANK = 3 * D_MODEL // 16
C_KV_RANK = D_MODEL // 8
C_NOPE = 128
C_ROPE = 64
C_SCALE = (C_NOPE + C_ROPE) ** -0.5
C_QK_PAD = 256
D_GROUPS = 4
D_GROUP_DIM = D_MODEL // 16
D_W = D_GROUPS * D_GROUP_DIM
CD_KV0 = C_Q_RANK
CD_D0 = C_Q_RANK + C_KV_RANK + C_ROPE
N_EXPERTS = 32
TOP_K = 4
D_EXPERT = 3 * D_MODEL // 16
SWIGLU_LIMIT = 7.0
SWIGLU_ALPHA = 1.702
LOG2_E = 1.4426950408889634

V7X_VMEM_BYTES = 64 * 1024 * 1024
VMEM_LIMIT_BYTES = V7X_VMEM_BYTES - 8 * 1024 * 1024
LANES = 128

ROW_TILE = 256
MOE_TM = 256
COMBINE_TB = 128
UP_CHUNKS = 3
DOWN_CHUNKS = 4
WEIGHT_DMA_PRIORITY = 1
MLA_TQ = 1024
MLA_TK = 1024
FFT_A = 64
FFT_B = 128


def _cparams(*sem):
    return pltpu.CompilerParams(dimension_semantics=sem, vmem_limit_bytes=VMEM_LIMIT_BYTES)


def _mm_body(n_a, has_bias, has_res, *refs):
    a_refs = refs[:n_a]
    w_refs = refs[n_a:2 * n_a]
    pos = 2 * n_a
    acc = None
    for a_ref, w_ref in zip(a_refs, w_refs):
        d = jnp.dot(a_ref[...], w_ref[...], preferred_element_type=F32)
        acc = d if acc is None else acc + d
    if has_bias:
        acc = acc + refs[pos][...]
        pos += 1
    if has_res:
        acc = refs[pos][...] + refs[pos + 1][...] * acc
    o_ref = refs[-1]
    o_ref[...] = acc.astype(o_ref.dtype)


def matmul(a_list, w_list, *, out_dtype, tm, tn, bias=None, res=None, gate=None, a_resident=False, name="mm"):
    m = a_list[0].shape[0]
    n = w_list[0].shape[1]
    assert m % tm == 0 and n % tn == 0, (m, tm, n, tn)
    if a_resident:
        grid = (m // tm, n // tn)
        im = lambda i, j: i
        jn = lambda i, j: j
    else:
        grid = (n // tn, m // tm)
        im = lambda j, i: i
        jn = lambda j, i: j
    in_specs = []
    for a in a_list:
        in_specs.append(pl.BlockSpec((tm, a.shape[1]), lambda *g: (im(*g), 0)))
    for w in w_list:
        in_specs.append(pl.BlockSpec((w.shape[0], tn), lambda *g: (0, jn(*g))))
    args = list(a_list) + list(w_list)
    if bias is not None:
        in_specs.append(pl.BlockSpec((1, tn), lambda *g: (0, jn(*g))))
        args.append(bias)
    if res is not None:
        in_specs.append(pl.BlockSpec((tm, tn), lambda *g: (im(*g), jn(*g))))
        in_specs.append(pl.BlockSpec((1, tn), lambda *g: (0, jn(*g))))
        args += [res, gate]
    return pl.pallas_call(
        functools.partial(_mm_body, len(a_list), bias is not None, res is not None),
        grid=grid,
        in_specs=in_specs,
        out_specs=pl.BlockSpec((tm, tn), lambda *g: (im(*g), jn(*g))),
        out_shape=jax.ShapeDtypeStruct((m, n), out_dtype),
        compiler_params=_cparams("parallel", "parallel"),
        name=name,
    )(*args)


def _mod_body(c_ref, w_ref, b_ref, o_ref):
    c = c_ref[...]
    a = (c * jax.nn.sigmoid(c)).astype(BF16)
    o_ref[...] = jnp.dot(a, w_ref[0].astype(BF16), preferred_element_type=F32) + b_ref[0]


def modulation(c_rows, w_mod, b_mod, layer, tn=1024):
    _, d, n = w_mod.shape
    return pl.pallas_call(
        _mod_body,
        grid=(n // tn,),
        in_specs=[
            pl.BlockSpec((8, d), lambda j: (0, 0)),
            pl.BlockSpec((1, d, tn), lambda j: (layer, 0, j)),
            pl.BlockSpec((1, 1, tn), lambda j: (layer, 0, j)),
        ],
        out_specs=pl.BlockSpec((8, tn), lambda j: (0, j)),
        out_shape=jax.ShapeDtypeStruct((8, n), F32),
        compiler_params=_cparams("parallel"),
        name="modulation",
    )(c_rows, w_mod, b_mod[:, None, :])


def _norm_mod_body(x_ref, g_ref, scale_ref, shift_ref, o_ref):
    x = x_ref[...].astype(F32)
    y = x * lax.rsqrt(jnp.mean(x * x, axis=-1, keepdims=True) + EPS)
    y = y * g_ref[...]
    o_ref[...] = (y * (1.0 + scale_ref[0]) + shift_ref[0]).astype(o_ref.dtype)


def norm_mod(x, g, scale_tbl, shift_tbl, *, out_dtype, rows=None, split_tile=None, tm=ROW_TILE, name="norm_mod"):
    d = x.shape[1]
    rows = x.shape[0] if rows is None else rows
    assert rows % tm == 0
    if split_tile is None:
        sel = lambda i: (0, 0, 0)
    else:
        sel = lambda i: (jnp.where(i >= split_tile, 1, 0), 0, 0)
    return pl.pallas_call(
        _norm_mod_body,
        grid=(rows // tm,),
        in_specs=[
            pl.BlockSpec((tm, d), lambda i: (i, 0)),
            pl.BlockSpec((1, d), lambda i: (0, 0)),
            pl.BlockSpec((1, 1, d), sel),
            pl.BlockSpec((1, 1, d), sel),
        ],
        out_specs=pl.BlockSpec((tm, d), lambda i: (i, 0)),
        out_shape=jax.ShapeDtypeStruct((rows, d), out_dtype),
        compiler_params=_cparams("parallel"),
        name=name,
    )(x, g[None].astype(F32), scale_tbl, shift_tbl)


def _gmlp_body(u_ref, v_ref, ws_ref, bs_ref, o_ref):
    for g in range(A_GROUPS):
        sl = slice(g * A_GROUP_DIM, (g + 1) * A_GROUP_DIM)
        ug = jax.nn.gelu(u_ref[:, sl].astype(F32))
        vf = jax.nn.gelu(v_ref[:, sl].astype(F32))
        mu = jnp.mean(vf, axis=-1, keepdims=True)
        vc = vf - mu
        var = jnp.mean(vc * vc, axis=-1, keepdims=True)
        vn = (vc * lax.rsqrt(var + EPS)).astype(BF16)
        mixed = jnp.dot(ws_ref[g], vn, preferred_element_type=F32) + bs_ref[g]
        o_ref[:, sl] = (ug * mixed).astype(o_ref.dtype)


def gmlp(z, ws_bf, bs_col):
    n = z.shape[0]
    return pl.pallas_call(
        _gmlp_body,
        grid=(n // CHUNK,),
        in_specs=[
            pl.BlockSpec((CHUNK, A_W), lambda i: (i, 0)),
            pl.BlockSpec((CHUNK, A_W), lambda i: (i, 1)),
            pl.BlockSpec((A_GROUPS, CHUNK, CHUNK), lambda i: (0, 0, 0)),
            pl.BlockSpec((A_GROUPS, CHUNK, 1), lambda i: (0, 0, 0)),
        ],
        out_specs=pl.BlockSpec((CHUNK, A_W), lambda i: (i, 0)),
        out_shape=jax.ShapeDtypeStruct((n, A_W), BF16),
        compiler_params=_cparams("parallel"),
        name="gmlp",
    )(z, z, ws_bf, bs_col)


def _swap_halves(x):
    lane = lax.broadcasted_iota(jnp.int32, x.shape, 1)
    return jnp.where((lane & 63) < 32, pltpu.roll(x, 96, 1), pltpu.roll(x, 32, 1))


def _rope_slab(x, cos, sin):
    return x * cos + _swap_halves(x) * sin


def _rope_q_body(z_ref, cos_ref, sin_ref, o_ref):
    cos = cos_ref[...]
    sin = sin_ref[...]
    for s in range(z_ref.shape[1] // LANES):
        sl = slice(s * LANES, (s + 1) * LANES)
        x = z_ref[:, sl].astype(F32)
        o_ref[:, sl] = (_rope_slab(x, cos, sin) * B_SCALE).astype(o_ref.dtype)


def rope_q(z, cos_t, sin_t, tm=ROW_TILE):
    n = z.shape[0]
    col0 = (2 * A_W) // B_QW
    return pl.pallas_call(
        _rope_q_body,
        grid=(n // tm,),
        in_specs=[
            pl.BlockSpec((tm, B_QW), lambda i: (i, col0)),
            pl.BlockSpec((tm, LANES), lambda i: (i, 0)),
            pl.BlockSpec((tm, LANES), lambda i: (i, 0)),
        ],
        out_specs=pl.BlockSpec((tm, B_QW), lambda i: (i, 0)),
        out_shape=jax.ShapeDtypeStruct((n, B_QW), BF16),
        compiler_params=_cparams("parallel"),
        name="rope_q",
    )(z, cos_t, sin_t)


def _dup_heads(x):
    lane = lax.broadcasted_iota(jnp.int32, x.shape, 1)
    r = pltpu.roll(x, 64, 1)
    lo = lane < 64
    return jnp.where(lo, x, r), jnp.where(lo, r, x)


def _rope_kv_body(z_ref, cos_ref, sin_ref, kd_ref, vd_ref):
    cos = cos_ref[...]
    sin = sin_ref[...]
    for s in range(B_KW // LANES):
        xk = z_ref[:, s * LANES:(s + 1) * LANES].astype(F32)
        e, o = _dup_heads(_rope_slab(xk, cos, sin))
        kd_ref[:, (2 * s) * LANES:(2 * s + 1) * LANES] = e.astype(kd_ref.dtype)
        kd_ref[:, (2 * s + 1) * LANES:(2 * s + 2) * LANES] = o.astype(kd_ref.dtype)
        xv = z_ref[:, B_KW + s * LANES:B_KW + (s + 1) * LANES].astype(F32)
        e, o = _dup_heads(xv)
        vd_ref[:, (2 * s) * LANES:(2 * s + 1) * LANES] = e.astype(vd_ref.dtype)
        vd_ref[:, (2 * s + 1) * LANES:(2 * s + 2) * LANES] = o.astype(vd_ref.dtype)


def rope_kv(z, cos_t, sin_t, tm=ROW_TILE):
    n = z.shape[0]
    wb = 2 * B_KW
    col0 = AB_KV0 // wb
    out_w = B_KV_HEADS * LANES
    return pl.pallas_call(
        _rope_kv_body,
        grid=(n // tm,),
        in_specs=[
            pl.BlockSpec((tm, wb), lambda i: (i, col0)),
            pl.BlockSpec((tm, LANES), lambda i: (i, 0)),
            pl.BlockSpec((tm, LANES), lambda i: (i, 0)),
        ],
        out_specs=[pl.BlockSpec((tm, out_w), lambda i: (i, 0)), pl.BlockSpec((tm, out_w), lambda i: (i, 0))],
        out_shape=[jax.ShapeDtypeStruct((n, out_w), BF16), jax.ShapeDtypeStruct((n, out_w), BF16)],
        compiler_params=_cparams("parallel"),
        name="rope_kv",
    )(z, cos_t, sin_t)


def _window_attn_body(has_window, nb, *refs):
    if has_window:
        q_ref, kp_ref, kc_ref, kn_ref, vp_ref, vc_ref, vn_ref, kx_ref, vx_ref, sink_ref, o_ref = refs
    else:
        q_ref, kx_ref, vx_ref, sink_ref, o_ref = refs
    i = pl.program_id(0)
    n_ctx = kx_ref.shape[0]
    lane = lax.broadcasted_iota(jnp.int32, (BLOCK, LANES), 1)
    lo = lane < 64
    if has_window:
        r = lax.broadcasted_iota(jnp.int32, (BLOCK, BLOCK), 0)
        c = lax.broadcasted_iota(jnp.int32, (BLOCK, BLOCK), 1)
        ones = jnp.ones((BLOCK, BLOCK), jnp.bool_)
        valid = jnp.concatenate(
            [(c >= r) & (i > 0), ones, (c <= r) & (i < nb - 1), jnp.ones((BLOCK, n_ctx), jnp.bool_)], axis=1)
    for j in range(B_KV_HEADS):
        hs = slice(j * LANES, (j + 1) * LANES)
        if has_window:
            kj = jnp.concatenate([kp_ref[:, hs], kc_ref[:, hs], kn_ref[:, hs], kx_ref[:, hs]], axis=0)
            vj = jnp.concatenate([vp_ref[:, hs], vc_ref[:, hs], vn_ref[:, hs], vx_ref[:, hs]], axis=0)
        else:
            kj = kx_ref[:, hs]
            vj = vx_ref[:, hs]
        rows = []
        for m in range(B_GROUP // 2):
            slab = q_ref[:, (j * (B_GROUP // 2) + m) * LANES:(j * (B_GROUP // 2) + m + 1) * LANES]
            zero = jnp.zeros_like(slab)
            rows += [jnp.where(lo, slab, zero), jnp.where(lo, zero, slab)]
        qs = jnp.concatenate(rows, axis=0)
        s = lax.dot_general(qs, kj, (((1,), (1,)), ((), ())), preferred_element_type=F32)
        s = s.reshape(B_GROUP, BLOCK, kj.shape[0])
        if has_window:
            s = jnp.where(valid[None], s, NEG_INF)
        sink = sink_ref[j]
        mx = jnp.maximum(jnp.max(s, axis=-1, keepdims=True), sink)
        p = jnp.exp(s - mx)
        denom = jnp.sum(p, axis=-1, keepdims=True) + jnp.exp(sink - mx)
        o = jnp.dot(p.reshape(B_GROUP * BLOCK, kj.shape[0]).astype(BF16), vj, preferred_element_type=F32)
        o = o.reshape(B_GROUP, BLOCK, LANES) / denom
        for m in range(B_GROUP // 2):
            col = (j * (B_GROUP // 2) + m) * LANES
            o_ref[:, col:col + LANES] = jnp.where(lo, o[2 * m], o[2 * m + 1]).astype(o_ref.dtype)


def window_attn(q, kd, vd, kd_ctx, vd_ctx, sink4, has_window):
    n = q.shape[0]
    nb = n // BLOCK
    n_ctx = kd_ctx.shape[0]
    kw = B_KV_HEADS * LANES
    blk = lambda f: pl.BlockSpec((BLOCK, kw), f)
    full_ctx = pl.BlockSpec((n_ctx, kw), lambda i: (0, 0))
    in_specs = [pl.BlockSpec((BLOCK, B_QW), lambda i: (i, 0))]
    args = [q]
    if has_window:
        prev = lambda i: (jnp.maximum(i - 1, 0), 0)
        cur = lambda i: (i, 0)
        nxt = lambda i: (jnp.minimum(i + 1, nb - 1), 0)
        in_specs += [blk(prev), blk(cur), blk(nxt), blk(prev), blk(cur), blk(nxt)]
        args += [kd, kd, kd, vd, vd, vd]
    in_specs += [full_ctx, full_ctx, pl.BlockSpec((B_KV_HEADS, B_GROUP, 1, 1), lambda i: (0, 0, 0, 0))]
    args += [kd_ctx, vd_ctx, sink4]
    return pl.pallas_call(
        functools.partial(_window_attn_body, has_window, nb),
        grid=(nb,),
        in_specs=in_specs,
        out_specs=pl.BlockSpec((BLOCK, B_QW), lambda i: (i, 0)),
        out_shape=jax.ShapeDtypeStruct((n, B_QW), BF16),
        compiler_params=_cparams("parallel"),
        name="window_attn" if has_window else "ctx_attn",
    )(*args)


def _mla_rope(u, cos, sin):
    return u * cos + _swap_halves(u) * sin


def _kvprep_body(z_ref, g_ref, cos_ref, sin_ref, o_ref):
    ckv = z_ref[:, :C_KV_RANK].astype(F32)
    y = ckv * lax.rsqrt(jnp.mean(ckv * ckv, axis=-1, keepdims=True) + EPS) * g_ref[...]
    o_ref[:, :C_KV_RANK] = y.astype(o_ref.dtype)
    u = z_ref[:, C_KV_RANK:].astype(F32)
    o_ref[:, C_KV_RANK:] = _mla_rope(u, cos_ref[...], sin_ref[...]).astype(o_ref.dtype)


def kvprep(ckvkr, g_ckv, cos_t, sin_t, tm=ROW_TILE):
    n, w = ckvkr.shape
    return pl.pallas_call(
        _kvprep_body,
        grid=(n // tm,),
        in_specs=[
            pl.BlockSpec((tm, w), lambda i: (i, 0)),
            pl.BlockSpec((1, C_KV_RANK), lambda i: (0, 0)),
            pl.BlockSpec((tm, LANES), lambda i: (i, 0)),
            pl.BlockSpec((tm, LANES), lambda i: (i, 0)),
        ],
        out_specs=pl.BlockSpec((tm, w), lambda i: (i, 0)),
        out_shape=jax.ShapeDtypeStruct((n, w), BF16),
        compiler_params=_cparams("parallel"),
        name="kvprep",
    )(ckvkr, g_ckv[None].astype(F32), cos_t, sin_t)


def _mla_body(tk, q_ref, cos_ref, sin_ref, k_ref, v_ref, kx_ref, vx_ref, o_ref):
    tq = q_ref.shape[0]
    sc = C_SCALE * LOG2_E
    qn = q_ref[:, :LANES].astype(F32) * sc
    qr = _mla_rope(q_ref[:, LANES:].astype(F32), cos_ref[...], sin_ref[...]) * sc
    q = jnp.concatenate([qn, qr], axis=1).astype(BF16)

    def ones_column(rows):
        lane = lax.broadcasted_iota(jnp.int32, (rows, LANES), 1)
        return jnp.where(lane == 0, 1.0, 0.0).astype(BF16)

    def step(k, v, carry):
        m, acc = carry
        s = lax.dot_general(q, k, (((1,), (1,)), ((), ())), preferred_element_type=F32)
        m_new = jnp.maximum(m, jnp.max(s, axis=-1, keepdims=True))
        alpha = jnp.exp2(m - m_new)
        p = jnp.exp2(s - m_new).astype(BF16)
        v1 = jnp.concatenate([v, ones_column(v.shape[0])], axis=1)
        return m_new, alpha * acc + jnp.dot(p, v1, preferred_element_type=F32)

    carry = (jnp.full((tq, 1), -jnp.inf, F32), jnp.zeros((tq, C_HEAD_V + LANES), F32))
    for c in range(k_ref.shape[0] // tk):
        carry = step(k_ref[c * tk:(c + 1) * tk, :], v_ref[c * tk:(c + 1) * tk, :], carry)
    m, acc = step(kx_ref[...], vx_ref[...], carry)
    o_ref[...] = (acc[:, :C_HEAD_V] / acc[:, C_HEAD_V:C_HEAD_V + 1]).astype(o_ref.dtype)


def mla_attn(qp, kv, kv_ctx, cos_t, sin_t, tq=MLA_TQ, tk=MLA_TK):
    n = qp.shape[0]
    n_ctx = kv_ctx.shape[0]
    v_col0 = (C_HEADS * C_QK_PAD) // C_HEAD_V
    return pl.pallas_call(
        functools.partial(_mla_body, tk),
        grid=(C_HEADS, n // tq),
        in_specs=[
            pl.BlockSpec((tq, C_QK_PAD), lambda h, i: (i, h)),
            pl.BlockSpec((tq, LANES), lambda h, i: (i, 0)),
            pl.BlockSpec((tq, LANES), lambda h, i: (i, 0)),
            pl.BlockSpec((n, C_QK_PAD), lambda h, i: (0, h)),
            pl.BlockSpec((n, C_HEAD_V), lambda h, i: (0, v_col0 + h)),
            pl.BlockSpec((n_ctx, C_QK_PAD), lambda h, i: (0, h)),
            pl.BlockSpec((n_ctx, C_HEAD_V), lambda h, i: (0, v_col0 + h)),
        ],
        out_specs=pl.BlockSpec((tq, C_HEAD_V), lambda h, i: (i, h)),
        out_shape=jax.ShapeDtypeStruct((n, C_HEADS * C_HEAD_V), BF16),
        compiler_params=_cparams("parallel", "parallel"),
        name="mla_attn",
    )(qp, cos_t, sin_t, kv, kv, kv_ctx, kv_ctx)


_HI = lax.Precision.HIGHEST


def _fft1_body(x_ref, cs_ref, tr_ref, ti_ref, o_ref):
    u = jnp.dot(cs_ref[...], x_ref[...], precision=_HI, preferred_element_type=F32)
    ur, ui = u[:FFT_A], u[FFT_A:]
    reps = x_ref.shape[1] // LANES
    tr = jnp.concatenate([tr_ref[0]] * reps, axis=1)
    ti = jnp.concatenate([ti_ref[0]] * reps, axis=1)
    o_ref[0] = tr * ur - ti * ui
    o_ref[1] = tr * ui + ti * ur


def _fft2_body(v_ref, w2_ref, cc_ref, sc_ref, o_ref):
    vv = jnp.concatenate([v_ref[0, 0], v_ref[1, 0]], axis=0)
    g = jnp.dot(w2_ref[...], vv, precision=_HI, preferred_element_type=F32)
    gr, gi = g[:FFT_B], g[FFT_B:]
    for grp in range(D_GROUPS):
        sl = slice(grp * D_GROUP_DIM, (grp + 1) * D_GROUP_DIM)
        y = (jnp.dot(gr[:, sl], cc_ref[...], precision=_HI, preferred_element_type=F32)
             + jnp.dot(gi[:, sl], sc_ref[...], precision=_HI, preferred_element_type=F32))
        o_ref[:, sl] = y.astype(o_ref.dtype)


def _dft_tables(n):
    a_n, b_n = FFT_A, FFT_B
    assert a_n * b_n == n
    two_pi = 2.0 * math.pi

    def ang(i, j, period):
        return (two_pi / period) * ((i[:, None] * j[None, :]) % period).astype(F32)

    ia = jnp.arange(a_n, dtype=jnp.int32)
    ib = jnp.arange(b_n, dtype=jnp.int32)
    ic = jnp.arange(D_GROUP_DIM, dtype=jnp.int32)
    th_a = ang(ia, ia, a_n)
    cs = jnp.concatenate([jnp.cos(th_a), -jnp.sin(th_a)], axis=0)
    th_t = ang(ib, ia, n)
    tr = jnp.broadcast_to(jnp.cos(th_t)[:, :, None], (b_n, a_n, LANES))
    ti = jnp.broadcast_to(-jnp.sin(th_t)[:, :, None], (b_n, a_n, LANES))
    th_b = ang(ib, ib, b_n)
    cb, sb = jnp.cos(th_b), jnp.sin(th_b)
    w2 = jnp.concatenate([jnp.concatenate([cb, sb], axis=1), jnp.concatenate([-sb, cb], axis=1)], axis=0)
    th_c = ang(ic, ic, D_GROUP_DIM)
    norm = 1.0 / math.sqrt(n * D_GROUP_DIM)
    return cs, tr, ti, w2, jnp.cos(th_c) * norm, jnp.sin(th_c) * norm


def fourier_mix(dz):
    n = dz.shape[0]
    a_n, b_n = FFT_A, FFT_B
    cs, tr, ti, w2, cc, sc = _dft_tables(n)
    x2 = dz.reshape(a_n, b_n * D_W)
    v = pl.pallas_call(
        _fft1_body,
        grid=(b_n,),
        in_specs=[
            pl.BlockSpec((a_n, D_W), lambda b: (0, b)),
            pl.BlockSpec((2 * a_n, a_n), lambda b: (0, 0)),
            pl.BlockSpec((1, a_n, LANES), lambda b: (b, 0, 0)),
            pl.BlockSpec((1, a_n, LANES), lambda b: (b, 0, 0)),
        ],
        out_specs=pl.BlockSpec((2, a_n, D_W), lambda b: (0, 0, b)),
        out_shape=jax.ShapeDtypeStruct((2, a_n, b_n * D_W), F32),
        compiler_params=_cparams("parallel"),
        name="fft_stage1",
    )(x2, cs, tr, ti)
    v4 = v.reshape(2, a_n, b_n, D_W)
    y = pl.pallas_call(
        _fft2_body,
        grid=(a_n,),
        in_specs=[
            pl.BlockSpec((2, 1, b_n, D_W), lambda k: (0, k, 0, 0)),
            pl.BlockSpec((2 * b_n, 2 * b_n), lambda k: (0, 0)),
            pl.BlockSpec((D_GROUP_DIM, D_GROUP_DIM), lambda k: (0, 0)),
            pl.BlockSpec((D_GROUP_DIM, D_GROUP_DIM), lambda k: (0, 0)),
        ],
        out_specs=pl.BlockSpec((b_n, D_W), lambda k: (0, k)),
        out_shape=jax.ShapeDtypeStruct((b_n, a_n * D_W), BF16),
        compiler_params=_cparams("parallel"),
        name="fft_stage2",
    )(v4, w2, cc, sc)
    return y.reshape(n, D_W)


def _router_body(h_ref, w_ref, b_ref, idx_ref, wgt_ref, rank_ref, cnt_ref, carry_ref):
    i = pl.program_id(0)
    tm = h_ref.shape[0]

    @pl.when(i == 0)
    def _():
        carry_ref[...] = jnp.zeros_like(carry_ref)

    logits = jnp.dot(h_ref[...], w_ref[0], precision=_HI, preferred_element_type=F32) + b_ref[0]
    e_iota = lax.broadcasted_iota(jnp.int32, (tm, N_EXPERTS), 1).astype(F32)
    k_iota = lax.broadcasted_iota(jnp.int32, (tm, TOP_K), 1)
    work = logits
    mask = jnp.zeros((tm, N_EXPERTS), F32)
    idx4 = jnp.zeros((tm, TOP_K), F32)
    val4 = jnp.zeros((tm, TOP_K), F32)
    sels = []
    for k in range(TOP_K):
        mk = jnp.max(work, axis=-1, keepdims=True)
        ik = jnp.min(jnp.where(work == mk, e_iota, float(N_EXPERTS)), axis=-1, keepdims=True)
        sel = e_iota == ik
        sels.append(sel)
        mask = mask + sel.astype(F32)
        idx4 = jnp.where(k_iota == k, ik, idx4)
        val4 = jnp.where(k_iota == k, mk, val4)
        work = jnp.where(sel, -jnp.inf, work)
    ex = jnp.exp(val4 - jnp.max(val4, axis=-1, keepdims=True))
    wgt_ref[...] = ex / jnp.sum(ex, axis=-1, keepdims=True)
    idx_ref[...] = idx4.astype(jnp.int32)
    r = lax.broadcasted_iota(jnp.int32, (tm, tm), 0)
    c = lax.broadcasted_iota(jnp.int32, (tm, tm), 1)
    ltri = jnp.where(r > c, 1.0, 0.0).astype(BF16)
    rank = jnp.dot(ltri, mask.astype(BF16), preferred_element_type=F32) + carry_ref[...]
    rank4 = jnp.zeros((tm, TOP_K), F32)
    for k in range(TOP_K):
        rk = jnp.sum(jnp.where(sels[k], rank, 0.0), axis=-1, keepdims=True)
        rank4 = jnp.where(k_iota == k, rk, rank4)
    rank_ref[...] = rank4.astype(jnp.int32)
    carry_ref[...] = carry_ref[...] + jnp.sum(mask, axis=0, keepdims=True)
    cnt_ref[...] = carry_ref[...]


def router(hf, w_router, b_router, layer, tm=ROW_TILE):
    t, d = hf.shape
    return pl.pallas_call(
        _router_body,
        grid=(t // tm,),
        in_specs=[
            pl.BlockSpec((tm, d), lambda i: (i, 0)),
            pl.BlockSpec((1, d, N_EXPERTS), lambda i: (layer, 0, 0)),
            pl.BlockSpec((1, 1, N_EXPERTS), lambda i: (layer, 0, 0)),
        ],
        out_specs=[
            pl.BlockSpec((tm, TOP_K), lambda i: (i, 0)),
            pl.BlockSpec((tm, TOP_K), lambda i: (i, 0)),
            pl.BlockSpec((tm, TOP_K), lambda i: (i, 0)),
            pl.BlockSpec((1, N_EXPERTS), lambda i: (0, 0)),
        ],
        out_shape=[
            jax.ShapeDtypeStruct((t, TOP_K), jnp.int32),
            jax.ShapeDtypeStruct((t, TOP_K), F32),
            jax.ShapeDtypeStruct((t, TOP_K), jnp.int32),
            jax.ShapeDtypeStruct((1, N_EXPERTS), F32),
        ],
        scratch_shapes=[pltpu.VMEM((1, N_EXPERTS), F32)],
        compiler_params=_cparams("arbitrary"),
        name="router",
    )(hf, w_router, b_router[:, None, :])


def _row_copy(src, src_row, dst, dst_row, sem):
    return pltpu.make_async_copy(src.at[pl.ds(src_row, 1)], dst.at[pl.ds(dst_row, 1)], sem)


def _first_tile_of_expert(te_ref, i):
    return (i == 0) | (te_ref[i] != te_ref[jnp.maximum(i - 1, 0)])


def _expert_up_body(layer, te_ref, nx_ref, nu_ref, tok_cur, tok_nxt, h_hbm, w_hbm, b_ref, o_ref,
                    xbuf, wst, wbf, sems, wsem):
    i = pl.program_id(0)
    nt = pl.num_programs(0)
    n_used = nu_ref[0]
    tm = xbuf.shape[1]

    def gather_start(tok_ref, slot, r):
        _row_copy(h_hbm, tok_ref[0, 0, r], xbuf.at[slot], r, sems.at[slot]).start()

    def gather_wait(slot):
        pltpu.make_async_copy(h_hbm.at[pl.ds(0, tm)], xbuf.at[slot], sems.at[slot]).wait()

    def weight_copy(e):
        return pltpu.make_async_copy(w_hbm.at[layer, e], wst, wsem)

    @pl.when(i == 0)
    def _():
        weight_copy(te_ref[0]).start(priority=WEIGHT_DMA_PRIORITY)

        def body(r, _):
            gather_start(tok_cur, 0, r)
            return 0
        lax.fori_loop(0, tm, body, 0)

    @pl.when(i < n_used)
    def _():
        slot = i % 2
        nxt = (i + 1) % 2
        gather_wait(slot)

        @pl.when(_first_tile_of_expert(te_ref, i))
        def _():
            weight_copy(te_ref[i]).wait()
            wbf[...] = wst[...].astype(BF16)

            @pl.when(nx_ref[i] >= 0)
            def _():
                weight_copy(nx_ref[i]).start(priority=WEIGHT_DMA_PRIORITY)

        x = xbuf[slot].astype(BF16)
        f = D_EXPERT
        cw = f // UP_CHUNKS
        per = -(-tm // UP_CHUNKS)
        bias = b_ref[0, 0]
        for c in range(UP_CHUNKS):
            for r in range(c * per, min(tm, (c + 1) * per)):
                gather_start(tok_nxt, nxt, r)
            gs = slice(c * cw, (c + 1) * cw)
            us = slice(f + c * cw, f + (c + 1) * cw)
            g = jnp.dot(x, wbf[:, gs], preferred_element_type=F32) + bias[:, gs]
            u = jnp.dot(x, wbf[:, us], preferred_element_type=F32) + bias[:, us]
            gate = jnp.minimum(g, SWIGLU_LIMIT)
            up = jnp.clip(u, -SWIGLU_LIMIT, SWIGLU_LIMIT)
            o_ref[:, gs] = ((up + 1.0) * (gate * jax.nn.sigmoid(SWIGLU_ALPHA * gate))).astype(o_ref.dtype)

        @pl.when(i == nt - 1)
        def _():
            gather_wait(nxt)

    @pl.when(i >= n_used)
    def _():
        @pl.when(i == n_used)
        def _():
            gather_wait(i % 2)

        o_ref[...] = jnp.zeros_like(o_ref)


def expert_up(hf, tok_tiles, tile_expert, next_expert, n_used, w_gu, b_gu, layer, tm=MOE_TM):
    nt = tok_tiles.shape[0]
    d = hf.shape[1]
    f2 = w_gu.shape[3]
    grid_spec = pltpu.PrefetchScalarGridSpec(
        num_scalar_prefetch=3,
        grid=(nt,),
        in_specs=[
            pl.BlockSpec((1, 1, tm), lambda i, *_: (i, 0, 0), memory_space=pltpu.SMEM),
            pl.BlockSpec((1, 1, tm), lambda i, *_: (jnp.minimum(i + 1, nt - 1), 0, 0), memory_space=pltpu.SMEM),
            pl.BlockSpec(memory_space=pl.ANY),
            pl.BlockSpec(memory_space=pl.ANY),
            pl.BlockSpec((1, 1, 1, f2), lambda i, te, nx, nu: (layer, te[i], 0, 0)),
        ],
        out_specs=pl.BlockSpec((tm, f2 // 2), lambda i, *_: (i, 0)),
        scratch_shapes=[
            pltpu.VMEM((2, tm, d), F32),
            pltpu.VMEM((d, f2), F32),
            pltpu.VMEM((d, f2), BF16),
            pltpu.SemaphoreType.DMA((2,)),
            pltpu.SemaphoreType.DMA(()),
        ],
    )
    return pl.pallas_call(
        functools.partial(_expert_up_body, layer),
        grid_spec=grid_spec,
        out_shape=jax.ShapeDtypeStruct((nt * tm, f2 // 2), BF16),
        compiler_params=_cparams("arbitrary"),
        name="expert_up",
    )(tile_expert, next_expert, n_used, tok_tiles, tok_tiles, hf, w_gu, b_gu[:, :, None, :])


def _expert_down_body(te_ref, nu_ref, dst_prev, dst_cur, a_ref, w_ref, b_ref, y_hbm, ybuf, wbf, sems):
    i = pl.program_id(0)
    n_used = nu_ref[0]
    tm = a_ref.shape[0]
    d = ybuf.shape[2]

    def scatter_start(dst_ref, slot, r, priority=0):
        _row_copy(ybuf.at[slot], r, y_hbm, dst_ref[0, 0, r], sems.at[slot]).start(priority=priority)

    def scatter_wait(slot):
        pltpu.make_async_copy(ybuf.at[slot], y_hbm.at[pl.ds(0, tm)], sems.at[slot]).wait()

    def compute(slot, prev, scatter_prev):
        a = a_ref[...]
        cw = d // DOWN_CHUNKS
        per = -(-tm // DOWN_CHUNKS)
        for c in range(DOWN_CHUNKS):
            if scatter_prev:
                for r in range(c * per, min(tm, (c + 1) * per)):
                    scatter_start(dst_prev, prev, r, priority=r % 2)
            cs = slice(c * cw, (c + 1) * cw)
            ybuf[slot, :, cs] = jnp.dot(a, wbf[:, cs], preferred_element_type=F32) + b_ref[0, 0][:, cs]

    @pl.when(i < n_used)
    def _():
        slot = i % 2
        prev = (i + 1) % 2

        @pl.when(i >= 2)
        def _():
            scatter_wait(slot)

        @pl.when(_first_tile_of_expert(te_ref, i))
        def _():
            wbf[...] = w_ref[0, 0].astype(BF16)

        @pl.when(i == 0)
        def _():
            compute(slot, prev, False)

        @pl.when(i > 0)
        def _():
            compute(slot, prev, True)

        @pl.when(i == n_used - 1)
        def _():
            def body(r, _):
                scatter_start(dst_cur, slot, r)
                return 0
            lax.fori_loop(0, tm, body, 0)

            @pl.when(i >= 1)
            def _():
                scatter_wait(prev)

            scatter_wait(slot)

    @pl.when(i >= n_used)
    def _():
        @pl.when(i == n_used)
        def _():
            ybuf[0] = jnp.zeros((tm, d), F32)

        zero_fill = pltpu.make_async_copy(ybuf.at[0], y_hbm.at[pl.ds(pl.multiple_of(i * tm, tm), tm)], sems.at[0])
        zero_fill.start()
        zero_fill.wait()


def expert_down(act, dst_tiles, tile_expert, n_used, w_down, b_down, layer, tm=MOE_TM):
    p, f = act.shape
    nt = p // tm
    d = w_down.shape[3]
    grid_spec = pltpu.PrefetchScalarGridSpec(
        num_scalar_prefetch=2,
        grid=(nt,),
        in_specs=[
            pl.BlockSpec((1, 1, tm), lambda i, te, nu: (jnp.maximum(i - 1, 0), 0, 0), memory_space=pltpu.SMEM),
            pl.BlockSpec((1, 1, tm), lambda i, te, nu: (i, 0, 0), memory_space=pltpu.SMEM),
            pl.BlockSpec((tm, f), lambda i, te, nu: (i, 0)),
            pl.BlockSpec((1, 1, f, d), lambda i, te, nu: (layer, te[i], 0, 0)),
            pl.BlockSpec((1, 1, 1, d), lambda i, te, nu: (layer, te[i], 0, 0)),
        ],
        out_specs=pl.BlockSpec(memory_space=pl.ANY),
        scratch_shapes=[pltpu.VMEM((2, tm, d), F32), pltpu.VMEM((f, d), BF16), pltpu.SemaphoreType.DMA((2,))],
    )
    return pl.pallas_call(
        _expert_down_body,
        grid_spec=grid_spec,
        out_shape=jax.ShapeDtypeStruct((p, d), F32),
        compiler_params=_cparams("arbitrary"),
        name="expert_down",
    )(tile_expert, n_used, dst_tiles, dst_tiles, act, w_down, b_down[:, :, None, :])


def _combine_body(wgt_ref, x_ref, gate_ref, y0_ref, y1_ref, y2_ref, y3_ref, o_ref):
    w = wgt_ref[...]
    f = w[:, 0:1] * y0_ref[...]
    for k, y_ref in enumerate((y1_ref, y2_ref, y3_ref), start=1):
        f = f + w[:, k:k + 1] * y_ref[...]
    o_ref[...] = x_ref[...] + gate_ref[0] * f


def combine(x_all, y_planes, wgt, gate_tbl, split_tile, tb=COMBINE_TB):
    t, d = x_all.shape
    nsteps = t // tb
    if split_tile is None:
        sel = lambda i: (0, 0, 0)
    else:
        sel = lambda i: (jnp.where(i >= split_tile, 1, 0), 0, 0)
    plane = lambda k: pl.BlockSpec((tb, d), lambda i: (k * nsteps + i, 0))
    return pl.pallas_call(
        _combine_body,
        grid=(nsteps,),
        in_specs=[
            pl.BlockSpec((tb, TOP_K), lambda i: (i, 0)),
            pl.BlockSpec((tb, d), lambda i: (i, 0)),
            pl.BlockSpec((1, 1, d), sel),
        ] + [plane(k) for k in range(TOP_K)],
        out_specs=pl.BlockSpec((tb, d), lambda i: (i, 0)),
        out_shape=jax.ShapeDtypeStruct((t, d), F32),
        compiler_params=_cparams("parallel"),
        name="moe_combine",
    )(wgt, x_all, gate_tbl, *([y_planes] * TOP_K))


def _routing_tables(idx4, rank4, counts, t, tm, nt):
    i32 = jnp.int32
    e_ids = jnp.arange(N_EXPERTS, dtype=i32)
    cnt = counts[0].astype(i32)
    tiles_e = (cnt + tm - 1) // tm
    tile_end = jnp.cumsum(tiles_e)
    offsets = (tile_end - tiles_e) * tm
    n_used = tile_end[-1]
    slot4 = jnp.sum(jnp.where(idx4[:, :, None] == e_ids, offsets, 0), axis=-1) + rank4
    te = jnp.sum((tile_end[None, :] <= jnp.arange(nt, dtype=i32)[:, None]).astype(i32), axis=1)
    last_e = jnp.sum(jnp.where(jnp.arange(nt, dtype=i32) == n_used - 1, te, 0))
    te = jnp.minimum(te, last_e)
    later_used = (tiles_e[None, :] > 0) & (e_ids[None, :] > e_ids[:, None])
    nxt_e = jnp.min(jnp.where(later_used, e_ids[None, :], N_EXPERTS), axis=1)
    nxt_e = jnp.where(nxt_e >= N_EXPERTS, -1, nxt_e)
    nx = jnp.sum(jnp.where(te[:, None] == e_ids, nxt_e, 0), axis=1).astype(i32)
    assign = jnp.arange(t * TOP_K, dtype=i32)
    code = jnp.full((nt * tm,), -1, i32).at[slot4.reshape(-1)].set(assign)
    is_pad = code < 0
    tok = jnp.where(is_pad, 0, code // TOP_K)
    pad_row = TOP_K * t + jnp.cumsum(is_pad.astype(i32)) - 1
    dst = jnp.where(is_pad, pad_row, (code % TOP_K) * t + code // TOP_K)
    return tok, dst, te.astype(i32), nx, n_used.astype(i32).reshape(1)


def moe_block(x_all, g_ffn, scale_tbl, shift_tbl, gate_tbl, split_rows, w_router, b_router, w_gu, b_gu,
              w_down, b_down, layer):
    t, d = x_all.shape
    tm = MOE_TM
    hf = norm_mod(x_all, g_ffn, scale_tbl, shift_tbl, out_dtype=F32,
                  split_tile=None if split_rows is None else split_rows // ROW_TILE, name="norm_ffn")
    idx4, wgt4, rank4, counts = router(hf, w_router, b_router, layer)
    nt = (t * TOP_K + N_EXPERTS * (tm - 1) + tm - 1) // tm
    tok, dst, te, nx, n_used = _routing_tables(idx4, rank4, counts, t, tm, nt)
    act = expert_up(hf, tok.reshape(nt, 1, tm), te, nx, n_used, w_gu, b_gu, layer)
    y_planes = expert_down(act, dst.reshape(nt, 1, tm), te, n_used, w_down, b_down, layer)
    return combine(x_all, y_planes, wgt4, gate_tbl, None if split_rows is None else split_rows // COMBINE_TB)


def _rope_angles(n_tokens, rot_dim):
    rows = n_tokens // GRID_W
    row = jnp.repeat(jnp.arange(rows, dtype=F32), GRID_W)
    col = jnp.tile(jnp.arange(GRID_W, dtype=F32), rows)
    n_freq = rot_dim // 4
    inv_freq = ROPE_THETA ** (-jnp.arange(n_freq, dtype=F32) / n_freq)
    ang = jnp.concatenate([row[:, None] * inv_freq, col[:, None] * inv_freq], axis=-1)
    return jnp.cos(ang), jnp.sin(ang)


def _rope_tables(n, n_ctx):
    cos, sin = _rope_angles(n, 64)
    z = jnp.zeros_like(cos)
    pair = (jnp.concatenate([cos] * 4, axis=1), jnp.concatenate([-sin, sin, -sin, sin], axis=1))
    single = (jnp.concatenate([cos, cos, z, z], axis=1), jnp.concatenate([-sin, sin, z, z], axis=1))
    ident = (jnp.ones((n_ctx, LANES), F32), jnp.zeros((n_ctx, LANES), F32))
    return pair, single, ident


def _pad_heads(w, n_heads, widths, pad_to):
    k = w.shape[0]
    per = sum(widths)
    w3 = w.reshape(k, n_heads, per)[:, :, :widths[0]]
    return jnp.pad(w3, ((0, 0), (0, 0), (0, pad_to - widths[0]))).reshape(k, n_heads * pad_to)


def _kv_weight(w_ukv_l):
    kin = C_KV_RANK + 2 * C_ROPE
    w_kv3 = w_ukv_l.reshape(C_KV_RANK, C_HEADS, C_NOPE + C_HEAD_V)
    w_k = jnp.pad(w_kv3[:, :, :C_NOPE], ((0, 0), (0, 0), (0, C_QK_PAD - C_NOPE)))
    rope_rows = jnp.pad(jnp.eye(C_ROPE, dtype=F32), ((0, 0), (C_NOPE, C_QK_PAD - C_NOPE - C_ROPE)))
    rope_rows = jnp.broadcast_to(rope_rows[:, None, :], (C_ROPE, C_HEADS, C_QK_PAD))
    w_k = jnp.concatenate([w_k, rope_rows, jnp.zeros((C_ROPE, C_HEADS, C_QK_PAD), F32)], axis=0)
    w_v = jnp.pad(w_kv3[:, :, C_NOPE:], ((0, 2 * C_ROPE), (0, 0), (0, 0)))
    return jnp.concatenate([w_k.reshape(kin, -1), w_v.reshape(kin, -1)], axis=1).astype(BF16)


def _tbl(lat_row, ctx_row=None):
    rows = [lat_row] if ctx_row is None else [lat_row, ctx_row]
    return jnp.stack(rows)[:, None, :]


def kernel(x, c, ctx, c_ctx, w_mod, b_mod, g_mix, g_ffn, w_in_ab, ws_a, bs_a, sink_b, w_out_ab, w_in_cd, g_cq,
           w_uq, g_ckv, w_ukv, w_out_cd, w_router, b_router, w_gu, b_gu, w_down, b_down, g_final):
    bsz, n, d = x.shape
    n_ctx = ctx.shape[1]
    assert bsz == 1 and d == D_MODEL and n == FFT_A * FFT_B
    xl = x[0]
    xc = ctx[0]
    c_rows = jnp.zeros((8, d), F32).at[0].set(c[0]).at[1].set(c_ctx)
    pair_t, single_t, ident_t = _rope_tables(n, n_ctx)
    zeros_tbl = jnp.zeros((1, 1, d), F32)

    def mod_rows(l):
        mod = modulation(c_rows, w_mod, b_mod, l)
        lat = [mod[0, k * d:(k + 1) * d] for k in range(N_MOD)]
        cx = [mod[1, k * d:(k + 1) * d] for k in range(N_MOD)]
        return lat, cx

    lat, cx = mod_rows(0)
    h = norm_mod(xl, g_mix[0], _tbl(lat[1]), _tbl(lat[0]), out_dtype=BF16, name="norm_mix0")
    hc = norm_mod(xc, g_mix[0], _tbl(cx[1]), _tbl(cx[0]), out_dtype=BF16, name="norm_mix0_ctx")
    w_in = w_in_ab[0].astype(BF16)
    z = matmul([h], [w_in], out_dtype=BF16, tm=1024, tn=512, a_resident=True, name="proj_ab")
    zc = matmul([hc], [w_in], out_dtype=BF16, tm=n_ctx, tn=512, a_resident=True, name="proj_ab_ctx")
    ws_bf = ws_a[0].astype(BF16)
    bs_col = bs_a[0][:, :, None]
    a_lat = gmlp(z, ws_bf, bs_col)
    a_ctx = gmlp(zc, ws_bf, bs_col)
    q_lat = rope_q(z, *pair_t)
    q_ctx = rope_q(zc, *ident_t)
    kd, vd = rope_kv(z, *pair_t)
    kd_c, vd_c = rope_kv(zc, *ident_t)
    sink4 = sink_b[0].astype(F32).reshape(B_KV_HEADS, B_GROUP, 1, 1)
    b_lat = window_attn(q_lat, kd, vd, kd_c, vd_c, sink4, True)
    b_ctx = window_attn(q_ctx, None, None, kd_c, vd_c, sink4, False)
    w_out = w_out_ab[0].astype(BF16)
    xl = matmul([a_lat, b_lat], [w_out[:A_W], w_out[A_W:]], out_dtype=F32, tm=512, tn=1024,
                res=xl, gate=lat[2][None], name="out_ab")
    xc = matmul([a_ctx, b_ctx], [w_out[:A_W], w_out[A_W:]], out_dtype=F32, tm=n_ctx, tn=1024,
                res=xc, gate=cx[2][None], name="out_ab_ctx")
    x_all = jnp.concatenate([xl, xc], axis=0)
    x_all = moe_block(x_all, g_ffn[0], _tbl(lat[4], cx[4]), _tbl(lat[3], cx[3]), _tbl(lat[5], cx[5]), n,
                      w_router, b_router, w_gu, b_gu, w_down, b_down, 0)
    xl = x_all
    xc = x_all[n:]

    lat, cx = mod_rows(1)
    h = norm_mod(xl, g_mix[1], _tbl(lat[1]), _tbl(lat[0]), out_dtype=BF16, rows=n, name="norm_mix1")
    hc = norm_mod(xc, g_mix[1], _tbl(cx[1]), _tbl(cx[0]), out_dtype=BF16, name="norm_mix1_ctx")
    w_in = w_in_cd[0]
    w_cq = w_in[:, :CD_KV0].astype(BF16)
    w_kvr = jnp.pad(w_in[:, CD_KV0:CD_D0], ((0, 0), (0, 64))).astype(BF16)
    w_dz = w_in[:, CD_D0:].astype(BF16)
    cq = matmul([h], [w_cq], out_dtype=BF16, tm=1024, tn=C_Q_RANK, name="proj_cq")
    ckvkr = matmul([h], [w_kvr], out_dtype=BF16, tm=1024, tn=640, name="proj_ckv")
    ckvkr_c = matmul([hc], [w_kvr], out_dtype=BF16, tm=n_ctx, tn=640, name="proj_ckv_ctx")
    dz = matmul([h], [w_dz], out_dtype=F32, tm=1024, tn=D_W, name="proj_dz")
    cqn = norm_mod(cq, g_cq[0], jnp.zeros((1, 1, C_Q_RANK), F32), jnp.zeros((1, 1, C_Q_RANK), F32),
                   out_dtype=BF16, name="norm_cq")
    w_uq_p = _pad_heads(w_uq[0], C_HEADS, (C_NOPE + C_ROPE,), C_QK_PAD).astype(BF16)
    qp = matmul([cqn], [w_uq_p], out_dtype=BF16, tm=1024, tn=1024, name="proj_uq")
    kv_in = kvprep(ckvkr, g_ckv[0], *single_t)
    kv_in_c = kvprep(ckvkr_c, g_ckv[0], *ident_t)
    w_kv = _kv_weight(w_ukv[0])
    kv = matmul([kv_in], [w_kv], out_dtype=BF16, tm=1024, tn=1024, name="proj_ukv")
    kv_c = matmul([kv_in_c], [w_kv], out_dtype=BF16, tm=n_ctx, tn=1024, name="proj_ukv_ctx")
    c_lat = mla_attn(qp, kv, kv_c, *single_t)
    d_lat = fourier_mix(dz)
    w_out = w_out_cd[0].astype(BF16)
    n_c = C_HEADS * C_HEAD_V
    xl = matmul([c_lat, d_lat], [w_out[:n_c], w_out[n_c:]], out_dtype=F32, tm=512, tn=1024,
                res=xl, gate=lat[2][None], name="out_cd")
    xl = moe_block(xl, g_ffn[1], _tbl(lat[4]), _tbl(lat[3]), _tbl(lat[5]), None,
                   w_router, b_router, w_gu, b_gu, w_down, b_down, 1)
    out = norm_mod(xl, g_final, zeros_tbl, zeros_tbl, out_dtype=F32, name="norm_final")
    return out[None]
```

```python
import functools
import math

import jax
import jax.numpy as jnp
from jax import lax
from jax.experimental import pallas as pl
from jax.experimental.pallas import tpu as pltpu

F32 = jnp.float32
BF16 = jnp.bfloat16

D_MODEL = 4096
GRID_W = 64
BLOCK = 128
CHUNK = 128
ROPE_THETA = 10000.0
EPS = 1e-6
NEG_INF = -1e30
N_MOD = 6
A_GROUPS = 8
A_GROUP_DIM = D_MODEL // 16
A_W = A_GROUPS * A_GROUP_DIM
B_HEAD_DIM = 64
B_HEADS = (D_MODEL // 2) // B_HEAD_DIM
B_KV_HEADS = B_HEADS // 8
B_GROUP = B_HEADS // B_KV_HEADS
B_QW = B_HEADS * B_HEAD_DIM
B_KW = B_KV_HEADS * B_HEAD_DIM
B_SCALE = B_HEAD_DIM ** -0.5
AB_KV0 = 2 * A_W + B_QW
C_HEAD_V = 128
C_HEADS = (3 * D_MODEL // 4) // C_HEAD_V
C_Q_RANK = 3 * D_MODEL // 16
C_KV_RANK = D_MODEL // 8
C_NOPE = 128
C_ROPE = 64
C_SCALE = (C_NOPE + C_ROPE) ** -0.5
C_QK_PAD = 256
D_GROUPS = 4
D_GROUP_DIM = D_MODEL // 16
D_W = D_GROUPS * D_GROUP_DIM
CD_KV0 = C_Q_RANK
CD_D0 = C_Q_RANK + C_KV_RANK + C_ROPE
N_EXPERTS = 32
TOP_K = 4
D_EXPERT = 3 * D_MODEL // 16
SWIGLU_LIMIT = 7.0
SWIGLU_ALPHA = 1.702
LOG2_E = 1.4426950408889634

V7X_VMEM_BYTES = 64 * 1024 * 1024
VMEM_LIMIT_BYTES = V7X_VMEM_BYTES - 8 * 1024 * 1024
LANES = 128

ROW_TILE = 256
MOE_TM = 256
COMBINE_TB = 128
UP_CHUNKS = 3
DOWN_CHUNKS = 4
WEIGHT_DMA_PRIORITY = 1
TOKEN_PITCH = 40
MLA_TQ = 1024
MLA_TK = 1024
FFT_A = 64
FFT_B = 128


def _cparams(*sem):
    return pltpu.CompilerParams(dimension_semantics=sem, vmem_limit_bytes=VMEM_LIMIT_BYTES)


def _mm_body(n_a, has_bias, has_res, *refs):
    a_refs = refs[:n_a]
    w_refs = refs[n_a:2 * n_a]
    pos = 2 * n_a
    acc = None
    for a_ref, w_ref in zip(a_refs, w_refs):
        d = jnp.dot(a_ref[...], w_ref[...], preferred_element_type=F32)
        acc = d if acc is None else acc + d
    if has_bias:
        acc = acc + refs[pos][...]
        pos += 1
    if has_res:
        acc = refs[pos][...] + refs[pos + 1][...] * acc
    o_ref = refs[-1]
    o_ref[...] = acc.astype(o_ref.dtype)


def matmul(a_list, w_list, *, out_dtype, tm, tn, bias=None, res=None, gate=None, a_resident=False, name="mm"):
    m = a_list[0].shape[0]
    n = w_list[0].shape[1]
    assert m % tm == 0 and n % tn == 0, (m, tm, n, tn)
    if a_resident:
        grid = (m // tm, n // tn)
        im = lambda i, j: i
        jn = lambda i, j: j
    else:
        grid = (n // tn, m // tm)
        im = lambda j, i: i
        jn = lambda j, i: j
    in_specs = []
    for a in a_list:
        in_specs.append(pl.BlockSpec((tm, a.shape[1]), lambda *g: (im(*g), 0)))
    for w in w_list:
        in_specs.append(pl.BlockSpec((w.shape[0], tn), lambda *g: (0, jn(*g))))
    args = list(a_list) + list(w_list)
    if bias is not None:
        in_specs.append(pl.BlockSpec((1, tn), lambda *g: (0, jn(*g))))
        args.append(bias)
    if res is not None:
        in_specs.append(pl.BlockSpec((tm, tn), lambda *g: (im(*g), jn(*g))))
        in_specs.append(pl.BlockSpec((1, tn), lambda *g: (0, jn(*g))))
        args += [res, gate]
    return pl.pallas_call(
        functools.partial(_mm_body, len(a_list), bias is not None, res is not None),
        grid=grid,
        in_specs=in_specs,
        out_specs=pl.BlockSpec((tm, tn), lambda *g: (im(*g), jn(*g))),
        out_shape=jax.ShapeDtypeStruct((m, n), out_dtype),
        compiler_params=_cparams("parallel", "parallel"),
        name=name,
    )(*args)


def _mod_body(c_ref, w_ref, b_ref, o_ref):
    c = c_ref[...]
    a = (c * jax.nn.sigmoid(c)).astype(BF16)
    o_ref[...] = jnp.dot(a, w_ref[0].astype(BF16), preferred_element_type=F32) + b_ref[0]


def modulation(c_rows, w_mod, b_mod, layer, tn=1024):
    _, d, n = w_mod.shape
    return pl.pallas_call(
        _mod_body,
        grid=(n // tn,),
        in_specs=[
            pl.BlockSpec((8, d), lambda j: (0, 0)),
            pl.BlockSpec((1, d, tn), lambda j: (layer, 0, j)),
            pl.BlockSpec((1, 1, tn), lambda j: (layer, 0, j)),
        ],
        out_specs=pl.BlockSpec((8, tn), lambda j: (0, j)),
        out_shape=jax.ShapeDtypeStruct((8, n), F32),
        compiler_params=_cparams("parallel"),
        name="modulation",
    )(c_rows, w_mod, b_mod[:, None, :])


def _norm_mod_body(x_ref, g_ref, scale_ref, shift_ref, o_ref):
    x = x_ref[...].astype(F32)
    y = x * lax.rsqrt(jnp.mean(x * x, axis=-1, keepdims=True) + EPS)
    y = y * g_ref[...]
    o_ref[...] = (y * (1.0 + scale_ref[0]) + shift_ref[0]).astype(o_ref.dtype)


def norm_mod(x, g, scale_tbl, shift_tbl, *, out_dtype, rows=None, split_tile=None, tm=ROW_TILE, name="norm_mod"):
    d = x.shape[1]
    rows = x.shape[0] if rows is None else rows
    assert rows % tm == 0
    if split_tile is None:
        sel = lambda i: (0, 0, 0)
    else:
        sel = lambda i: (jnp.where(i >= split_tile, 1, 0), 0, 0)
    return pl.pallas_call(
        _norm_mod_body,
        grid=(rows // tm,),
        in_specs=[
            pl.BlockSpec((tm, d), lambda i: (i, 0)),
            pl.BlockSpec((1, d), lambda i: (0, 0)),
            pl.BlockSpec((1, 1, d), sel),
            pl.BlockSpec((1, 1, d), sel),
        ],
        out_specs=pl.BlockSpec((tm, d), lambda i: (i, 0)),
        out_shape=jax.ShapeDtypeStruct((rows, d), out_dtype),
        compiler_params=_cparams("parallel"),
        name=name,
    )(x, g[None].astype(F32), scale_tbl, shift_tbl)


def _gmlp_body(u_ref, v_ref, ws_ref, bs_ref, o_ref):
    for g in range(A_GROUPS):
        sl = slice(g * A_GROUP_DIM, (g + 1) * A_GROUP_DIM)
        ug = jax.nn.gelu(u_ref[:, sl].astype(F32))
        vf = jax.nn.gelu(v_ref[:, sl].astype(F32))
        mu = jnp.mean(vf, axis=-1, keepdims=True)
        vc = vf - mu
        var = jnp.mean(vc * vc, axis=-1, keepdims=True)
        vn = (vc * lax.rsqrt(var + EPS)).astype(BF16)
        mixed = jnp.dot(ws_ref[g], vn, preferred_element_type=F32) + bs_ref[g]
        o_ref[:, sl] = (ug * mixed).astype(o_ref.dtype)


def gmlp(z, ws_bf, bs_col):
    n = z.shape[0]
    return pl.pallas_call(
        _gmlp_body,
        grid=(n // CHUNK,),
        in_specs=[
            pl.BlockSpec((CHUNK, A_W), lambda i: (i, 0)),
            pl.BlockSpec((CHUNK, A_W), lambda i: (i, 1)),
            pl.BlockSpec((A_GROUPS, CHUNK, CHUNK), lambda i: (0, 0, 0)),
            pl.BlockSpec((A_GROUPS, CHUNK, 1), lambda i: (0, 0, 0)),
        ],
        out_specs=pl.BlockSpec((CHUNK, A_W), lambda i: (i, 0)),
        out_shape=jax.ShapeDtypeStruct((n, A_W), BF16),
        compiler_params=_cparams("parallel"),
        name="gmlp",
    )(z, z, ws_bf, bs_col)


def _swap_halves(x):
    lane = lax.broadcasted_iota(jnp.int32, x.shape, 1)
    return jnp.where((lane & 63) < 32, pltpu.roll(x, 96, 1), pltpu.roll(x, 32, 1))


def _rope_slab(x, cos, sin):
    return x * cos + _swap_halves(x) * sin


def _rope_q_body(z_ref, cos_ref, sin_ref, o_ref):
    cos = cos_ref[...]
    sin = sin_ref[...]
    for s in range(z_ref.shape[1] // LANES):
        sl = slice(s * LANES, (s + 1) * LANES)
        x = z_ref[:, sl].astype(F32)
        o_ref[:, sl] = (_rope_slab(x, cos, sin) * B_SCALE).astype(o_ref.dtype)


def rope_q(z, cos_t, sin_t, tm=ROW_TILE):
    n = z.shape[0]
    col0 = (2 * A_W) // B_QW
    return pl.pallas_call(
        _rope_q_body,
        grid=(n // tm,),
        in_specs=[
            pl.BlockSpec((tm, B_QW), lambda i: (i, col0)),
            pl.BlockSpec((tm, LANES), lambda i: (i, 0)),
            pl.BlockSpec((tm, LANES), lambda i: (i, 0)),
        ],
        out_specs=pl.BlockSpec((tm, B_QW), lambda i: (i, 0)),
        out_shape=jax.ShapeDtypeStruct((n, B_QW), BF16),
        compiler_params=_cparams("parallel"),
        name="rope_q",
    )(z, cos_t, sin_t)


def _dup_heads(x):
    lane = lax.broadcasted_iota(jnp.int32, x.shape, 1)
    r = pltpu.roll(x, 64, 1)
    lo = lane < 64
    return jnp.where(lo, x, r), jnp.where(lo, r, x)


def _rope_kv_body(z_ref, cos_ref, sin_ref, kd_ref, vd_ref):
    cos = cos_ref[...]
    sin = sin_ref[...]
    for s in range(B_KW // LANES):
        xk = z_ref[:, s * LANES:(s + 1) * LANES].astype(F32)
        e, o = _dup_heads(_rope_slab(xk, cos, sin))
        kd_ref[:, (2 * s) * LANES:(2 * s + 1) * LANES] = e.astype(kd_ref.dtype)
        kd_ref[:, (2 * s + 1) * LANES:(2 * s + 2) * LANES] = o.astype(kd_ref.dtype)
        xv = z_ref[:, B_KW + s * LANES:B_KW + (s + 1) * LANES].astype(F32)
        e, o = _dup_heads(xv)
        vd_ref[:, (2 * s) * LANES:(2 * s + 1) * LANES] = e.astype(vd_ref.dtype)
        vd_ref[:, (2 * s + 1) * LANES:(2 * s + 2) * LANES] = o.astype(vd_ref.dtype)


def rope_kv(z, cos_t, sin_t, tm=ROW_TILE):
    n = z.shape[0]
    wb = 2 * B_KW
    col0 = AB_KV0 // wb
    out_w = B_KV_HEADS * LANES
    return pl.pallas_call(
        _rope_kv_body,
        grid=(n // tm,),
        in_specs=[
            pl.BlockSpec((tm, wb), lambda i: (i, col0)),
            pl.BlockSpec((tm, LANES), lambda i: (i, 0)),
            pl.BlockSpec((tm, LANES), lambda i: (i, 0)),
        ],
        out_specs=[pl.BlockSpec((tm, out_w), lambda i: (i, 0)), pl.BlockSpec((tm, out_w), lambda i: (i, 0))],
        out_shape=[jax.ShapeDtypeStruct((n, out_w), BF16), jax.ShapeDtypeStruct((n, out_w), BF16)],
        compiler_params=_cparams("parallel"),
        name="rope_kv",
    )(z, cos_t, sin_t)


def _window_attn_body(has_window, nb, *refs):
    if has_window:
        q_ref, kp_ref, kc_ref, kn_ref, vp_ref, vc_ref, vn_ref, kx_ref, vx_ref, sink_ref, o_ref = refs
    else:
        q_ref, kx_ref, vx_ref, sink_ref, o_ref = refs
    i = pl.program_id(0)
    n_ctx = kx_ref.shape[0]
    lane = lax.broadcasted_iota(jnp.int32, (BLOCK, LANES), 1)
    lo = lane < 64
    if has_window:
        r = lax.broadcasted_iota(jnp.int32, (BLOCK, BLOCK), 0)
        c = lax.broadcasted_iota(jnp.int32, (BLOCK, BLOCK), 1)
        ones = jnp.ones((BLOCK, BLOCK), jnp.bool_)
        valid = jnp.concatenate(
            [(c >= r) & (i > 0), ones, (c <= r) & (i < nb - 1), jnp.ones((BLOCK, n_ctx), jnp.bool_)], axis=1)
    for j in range(B_KV_HEADS):
        hs = slice(j * LANES, (j + 1) * LANES)
        if has_window:
            kj = jnp.concatenate([kp_ref[:, hs], kc_ref[:, hs], kn_ref[:, hs], kx_ref[:, hs]], axis=0)
            vj = jnp.concatenate([vp_ref[:, hs], vc_ref[:, hs], vn_ref[:, hs], vx_ref[:, hs]], axis=0)
        else:
            kj = kx_ref[:, hs]
            vj = vx_ref[:, hs]
        rows = []
        for m in range(B_GROUP // 2):
            slab = q_ref[:, (j * (B_GROUP // 2) + m) * LANES:(j * (B_GROUP // 2) + m + 1) * LANES]
            zero = jnp.zeros_like(slab)
            rows += [jnp.where(lo, slab, zero), jnp.where(lo, zero, slab)]
        qs = jnp.concatenate(rows, axis=0)
        s = lax.dot_general(qs, kj, (((1,), (1,)), ((), ())), preferred_element_type=F32)
        s = s.reshape(B_GROUP, BLOCK, kj.shape[0])
        if has_window:
            s = jnp.where(valid[None], s, NEG_INF)
        sink = sink_ref[j]
        mx = jnp.maximum(jnp.max(s, axis=-1, keepdims=True), sink)
        p = jnp.exp(s - mx)
        denom = jnp.sum(p, axis=-1, keepdims=True) + jnp.exp(sink - mx)
        o = jnp.dot(p.reshape(B_GROUP * BLOCK, kj.shape[0]).astype(BF16), vj, preferred_element_type=F32)
        o = o.reshape(B_GROUP, BLOCK, LANES) / denom
        for m in range(B_GROUP // 2):
            col = (j * (B_GROUP // 2) + m) * LANES
            o_ref[:, col:col + LANES] = jnp.where(lo, o[2 * m], o[2 * m + 1]).astype(o_ref.dtype)


def window_attn(q, kd, vd, kd_ctx, vd_ctx, sink4, has_window):
    n = q.shape[0]
    nb = n // BLOCK
    n_ctx = kd_ctx.shape[0]
    kw = B_KV_HEADS * LANES
    blk = lambda f: pl.BlockSpec((BLOCK, kw), f)
    full_ctx = pl.BlockSpec((n_ctx, kw), lambda i: (0, 0))
    in_specs = [pl.BlockSpec((BLOCK, B_QW), lambda i: (i, 0))]
    args = [q]
    if has_window:
        prev = lambda i: (jnp.maximum(i - 1, 0), 0)
        cur = lambda i: (i, 0)
        nxt = lambda i: (jnp.minimum(i + 1, nb - 1), 0)
        in_specs += [blk(prev), blk(cur), blk(nxt), blk(prev), blk(cur), blk(nxt)]
        args += [kd, kd, kd, vd, vd, vd]
    in_specs += [full_ctx, full_ctx, pl.BlockSpec((B_KV_HEADS, B_GROUP, 1, 1), lambda i: (0, 0, 0, 0))]
    args += [kd_ctx, vd_ctx, sink4]
    return pl.pallas_call(
        functools.partial(_window_attn_body, has_window, nb),
        grid=(nb,),
        in_specs=in_specs,
        out_specs=pl.BlockSpec((BLOCK, B_QW), lambda i: (i, 0)),
        out_shape=jax.ShapeDtypeStruct((n, B_QW), BF16),
        compiler_params=_cparams("parallel"),
        name="window_attn" if has_window else "ctx_attn",
    )(*args)


def _mla_rope(u, cos, sin):
    return u * cos + _swap_halves(u) * sin


def _kvprep_body(z_ref, g_ref, cos_ref, sin_ref, o_ref):
    ckv = z_ref[:, :C_KV_RANK].astype(F32)
    y = ckv * lax.rsqrt(jnp.mean(ckv * ckv, axis=-1, keepdims=True) + EPS) * g_ref[...]
    o_ref[:, :C_KV_RANK] = y.astype(o_ref.dtype)
    u = z_ref[:, C_KV_RANK:].astype(F32)
    o_ref[:, C_KV_RANK:] = _mla_rope(u, cos_ref[...], sin_ref[...]).astype(o_ref.dtype)


def kvprep(ckvkr, g_ckv, cos_t, sin_t, tm=ROW_TILE):
    n, w = ckvkr.shape
    return pl.pallas_call(
        _kvprep_body,
        grid=(n // tm,),
        in_specs=[
            pl.BlockSpec((tm, w), lambda i: (i, 0)),
            pl.BlockSpec((1, C_KV_RANK), lambda i: (0, 0)),
            pl.BlockSpec((tm, LANES), lambda i: (i, 0)),
            pl.BlockSpec((tm, LANES), lambda i: (i, 0)),
        ],
        out_specs=pl.BlockSpec((tm, w), lambda i: (i, 0)),
        out_shape=jax.ShapeDtypeStruct((n, w), BF16),
        compiler_params=_cparams("parallel"),
        name="kvprep",
    )(ckvkr, g_ckv[None].astype(F32), cos_t, sin_t)


def _mla_body(tk, q_ref, cos_ref, sin_ref, k_ref, v_ref, kx_ref, vx_ref, o_ref):
    tq = q_ref.shape[0]
    sc = C_SCALE * LOG2_E
    qn = q_ref[:, :LANES].astype(F32) * sc
    qr = _mla_rope(q_ref[:, LANES:].astype(F32), cos_ref[...], sin_ref[...]) * sc
    q = jnp.concatenate([qn, qr], axis=1).astype(BF16)

    def ones_column(rows):
        lane = lax.broadcasted_iota(jnp.int32, (rows, LANES), 1)
        return jnp.where(lane == 0, 1.0, 0.0).astype(BF16)

    def step(k, v, carry):
        m, acc = carry
        s = lax.dot_general(q, k, (((1,), (1,)), ((), ())), preferred_element_type=F32)
        m_new = jnp.maximum(m, jnp.max(s, axis=-1, keepdims=True))
        alpha = jnp.exp2(m - m_new)
        p = jnp.exp2(s - m_new).astype(BF16)
        v1 = jnp.concatenate([v, ones_column(v.shape[0])], axis=1)
        return m_new, alpha * acc + jnp.dot(p, v1, preferred_element_type=F32)

    carry = (jnp.full((tq, 1), -jnp.inf, F32), jnp.zeros((tq, C_HEAD_V + LANES), F32))
    for c in range(k_ref.shape[0] // tk):
        carry = step(k_ref[c * tk:(c + 1) * tk, :], v_ref[c * tk:(c + 1) * tk, :], carry)
    m, acc = step(kx_ref[...], vx_ref[...], carry)
    o_ref[...] = (acc[:, :C_HEAD_V] / acc[:, C_HEAD_V:C_HEAD_V + 1]).astype(o_ref.dtype)


def mla_attn(qp, kv, kv_ctx, cos_t, sin_t, tq=MLA_TQ, tk=MLA_TK):
    n = qp.shape[0]
    n_ctx = kv_ctx.shape[0]
    v_col0 = (C_HEADS * C_QK_PAD) // C_HEAD_V
    return pl.pallas_call(
        functools.partial(_mla_body, tk),
        grid=(C_HEADS, n // tq),
        in_specs=[
            pl.BlockSpec((tq, C_QK_PAD), lambda h, i: (i, h)),
            pl.BlockSpec((tq, LANES), lambda h, i: (i, 0)),
            pl.BlockSpec((tq, LANES), lambda h, i: (i, 0)),
            pl.BlockSpec((n, C_QK_PAD), lambda h, i: (0, h)),
            pl.BlockSpec((n, C_HEAD_V), lambda h, i: (0, v_col0 + h)),
            pl.BlockSpec((n_ctx, C_QK_PAD), lambda h, i: (0, h)),
            pl.BlockSpec((n_ctx, C_HEAD_V), lambda h, i: (0, v_col0 + h)),
        ],
        out_specs=pl.BlockSpec((tq, C_HEAD_V), lambda h, i: (i, h)),
        out_shape=jax.ShapeDtypeStruct((n, C_HEADS * C_HEAD_V), BF16),
        compiler_params=_cparams("parallel", "parallel"),
        name="mla_attn",
    )(qp, cos_t, sin_t, kv, kv, kv_ctx, kv_ctx)


_HI = lax.Precision.HIGHEST


def _fft1_body(x_ref, cs_ref, tr_ref, ti_ref, o_ref):
    u = jnp.dot(cs_ref[...], x_ref[...], precision=_HI, preferred_element_type=F32)
    ur, ui = u[:FFT_A], u[FFT_A:]
    reps = x_ref.shape[1] // LANES
    tr = jnp.concatenate([tr_ref[0]] * reps, axis=1)
    ti = jnp.concatenate([ti_ref[0]] * reps, axis=1)
    o_ref[0] = tr * ur - ti * ui
    o_ref[1] = tr * ui + ti * ur


def _fft2_body(v_ref, w2_ref, cc_ref, sc_ref, o_ref):
    vv = jnp.concatenate([v_ref[0, 0], v_ref[1, 0]], axis=0)
    g = jnp.dot(w2_ref[...], vv, precision=_HI, preferred_element_type=F32)
    gr, gi = g[:FFT_B], g[FFT_B:]
    for grp in range(D_GROUPS):
        sl = slice(grp * D_GROUP_DIM, (grp + 1) * D_GROUP_DIM)
        y = (jnp.dot(gr[:, sl], cc_ref[...], precision=_HI, preferred_element_type=F32)
             + jnp.dot(gi[:, sl], sc_ref[...], precision=_HI, preferred_element_type=F32))
        o_ref[:, sl] = y.astype(o_ref.dtype)


def _dft_tables(n):
    a_n, b_n = FFT_A, FFT_B
    assert a_n * b_n == n
    two_pi = 2.0 * math.pi

    def ang(i, j, period):
        return (two_pi / period) * ((i[:, None] * j[None, :]) % period).astype(F32)

    ia = jnp.arange(a_n, dtype=jnp.int32)
    ib = jnp.arange(b_n, dtype=jnp.int32)
    ic = jnp.arange(D_GROUP_DIM, dtype=jnp.int32)
    th_a = ang(ia, ia, a_n)
    cs = jnp.concatenate([jnp.cos(th_a), -jnp.sin(th_a)], axis=0)
    th_t = ang(ib, ia, n)
    tr = jnp.broadcast_to(jnp.cos(th_t)[:, :, None], (b_n, a_n, LANES))
    ti = jnp.broadcast_to(-jnp.sin(th_t)[:, :, None], (b_n, a_n, LANES))
    th_b = ang(ib, ib, b_n)
    cb, sb = jnp.cos(th_b), jnp.sin(th_b)
    w2 = jnp.concatenate([jnp.concatenate([cb, sb], axis=1), jnp.concatenate([-sb, cb], axis=1)], axis=0)
    th_c = ang(ic, ic, D_GROUP_DIM)
    norm = 1.0 / math.sqrt(n * D_GROUP_DIM)
    return cs, tr, ti, w2, jnp.cos(th_c) * norm, jnp.sin(th_c) * norm


def fourier_mix(dz):
    n = dz.shape[0]
    a_n, b_n = FFT_A, FFT_B
    cs, tr, ti, w2, cc, sc = _dft_tables(n)
    x2 = dz.reshape(a_n, b_n * D_W)
    v = pl.pallas_call(
        _fft1_body,
        grid=(b_n,),
        in_specs=[
            pl.BlockSpec((a_n, D_W), lambda b: (0, b)),
            pl.BlockSpec((2 * a_n, a_n), lambda b: (0, 0)),
            pl.BlockSpec((1, a_n, LANES), lambda b: (b, 0, 0)),
            pl.BlockSpec((1, a_n, LANES), lambda b: (b, 0, 0)),
        ],
        out_specs=pl.BlockSpec((2, a_n, D_W), lambda b: (0, 0, b)),
        out_shape=jax.ShapeDtypeStruct((2, a_n, b_n * D_W), F32),
        compiler_params=_cparams("parallel"),
        name="fft_stage1",
    )(x2, cs, tr, ti)
    v4 = v.reshape(2, a_n, b_n, D_W)
    y = pl.pallas_call(
        _fft2_body,
        grid=(a_n,),
        in_specs=[
            pl.BlockSpec((2, 1, b_n, D_W), lambda k: (0, k, 0, 0)),
            pl.BlockSpec((2 * b_n, 2 * b_n), lambda k: (0, 0)),
            pl.BlockSpec((D_GROUP_DIM, D_GROUP_DIM), lambda k: (0, 0)),
            pl.BlockSpec((D_GROUP_DIM, D_GROUP_DIM), lambda k: (0, 0)),
        ],
        out_specs=pl.BlockSpec((b_n, D_W), lambda k: (0, k)),
        out_shape=jax.ShapeDtypeStruct((b_n, a_n * D_W), BF16),
        compiler_params=_cparams("parallel"),
        name="fft_stage2",
    )(v4, w2, cc, sc)
    return y.reshape(n, D_W)


def _router_body(x_ref, g_ref, scale_ref, shift_ref, w_ref, b_ref, hs_ref, idx_ref, wgt_ref, rank_ref, cnt_ref,
                 carry_ref):
    i = pl.program_id(0)
    tm = x_ref.shape[0]
    n_chunks = x_ref.shape[1] // LANES

    @pl.when(i == 0)
    def _():
        carry_ref[...] = jnp.zeros_like(carry_ref)

    x = x_ref[...]
    y = x * lax.rsqrt(jnp.mean(x * x, axis=-1, keepdims=True) + EPS) * g_ref[...]
    hf = y * (1.0 + scale_ref[0]) + shift_ref[0]
    for c in range(TOKEN_PITCH):
        piece = hf[:, c * LANES:(c + 1) * LANES] if c < n_chunks else jnp.zeros((tm, LANES), F32)
        hs_ref[pl.ds(c, tm, stride=TOKEN_PITCH), :] = piece
    logits = jnp.dot(hf, w_ref[0], precision=_HI, preferred_element_type=F32) + b_ref[0]
    e_iota = lax.broadcasted_iota(jnp.int32, (tm, N_EXPERTS), 1).astype(F32)
    k_iota = lax.broadcasted_iota(jnp.int32, (tm, TOP_K), 1)
    work = logits
    mask = jnp.zeros((tm, N_EXPERTS), F32)
    idx4 = jnp.zeros((tm, TOP_K), F32)
    val4 = jnp.zeros((tm, TOP_K), F32)
    sels = []
    for k in range(TOP_K):
        mk = jnp.max(work, axis=-1, keepdims=True)
        ik = jnp.min(jnp.where(work == mk, e_iota, float(N_EXPERTS)), axis=-1, keepdims=True)
        sel = e_iota == ik
        sels.append(sel)
        mask = mask + sel.astype(F32)
        idx4 = jnp.where(k_iota == k, ik, idx4)
        val4 = jnp.where(k_iota == k, mk, val4)
        work = jnp.where(sel, -jnp.inf, work)
    ex = jnp.exp(val4 - jnp.max(val4, axis=-1, keepdims=True))
    wgt_ref[...] = ex / jnp.sum(ex, axis=-1, keepdims=True)
    idx_ref[...] = idx4.astype(jnp.int32)
    r = lax.broadcasted_iota(jnp.int32, (tm, tm), 0)
    c = lax.broadcasted_iota(jnp.int32, (tm, tm), 1)
    ltri = jnp.where(r > c, 1.0, 0.0).astype(BF16)
    rank = jnp.dot(ltri, mask.astype(BF16), preferred_element_type=F32) + carry_ref[...]
    rank4 = jnp.zeros((tm, TOP_K), F32)
    for k in range(TOP_K):
        rk = jnp.sum(jnp.where(sels[k], rank, 0.0), axis=-1, keepdims=True)
        rank4 = jnp.where(k_iota == k, rk, rank4)
    rank_ref[...] = rank4.astype(jnp.int32)
    carry_ref[...] = carry_ref[...] + jnp.sum(mask, axis=0, keepdims=True)
    cnt_ref[...] = carry_ref[...]


def router(x_all, g, scale_tbl, shift_tbl, split_tile, w_router, b_router, layer, tm=ROW_TILE):
    t, d = x_all.shape
    if split_tile is None:
        sel = lambda i: (0, 0, 0)
    else:
        sel = lambda i: (jnp.where(i >= split_tile, 1, 0), 0, 0)
    return pl.pallas_call(
        _router_body,
        grid=(t // tm,),
        in_specs=[
            pl.BlockSpec((tm, d), lambda i: (i, 0)),
            pl.BlockSpec((1, d), lambda i: (0, 0)),
            pl.BlockSpec((1, 1, d), sel),
            pl.BlockSpec((1, 1, d), sel),
            pl.BlockSpec((1, d, N_EXPERTS), lambda i: (layer, 0, 0)),
            pl.BlockSpec((1, 1, N_EXPERTS), lambda i: (layer, 0, 0)),
        ],
        out_specs=[
            pl.BlockSpec((tm * TOKEN_PITCH, LANES), lambda i: (i, 0)),
            pl.BlockSpec((tm, TOP_K), lambda i: (i, 0)),
            pl.BlockSpec((tm, TOP_K), lambda i: (i, 0)),
            pl.BlockSpec((tm, TOP_K), lambda i: (i, 0)),
            pl.BlockSpec((1, N_EXPERTS), lambda i: (0, 0)),
        ],
        out_shape=[
            jax.ShapeDtypeStruct((t * TOKEN_PITCH, LANES), F32),
            jax.ShapeDtypeStruct((t, TOP_K), jnp.int32),
            jax.ShapeDtypeStruct((t, TOP_K), F32),
            jax.ShapeDtypeStruct((t, TOP_K), jnp.int32),
            jax.ShapeDtypeStruct((1, N_EXPERTS), F32),
        ],
        scratch_shapes=[pltpu.VMEM((1, N_EXPERTS), F32)],
        compiler_params=_cparams("arbitrary"),
        name="router",
    )(x_all, g[None].astype(F32), scale_tbl, shift_tbl, w_router, b_router[:, None, :])


def _row_copy(src, src_row, dst, dst_row, sem):
    return pltpu.make_async_copy(src.at[pl.ds(src_row, 1)], dst.at[pl.ds(dst_row, 1)], sem)


def _first_tile_of_expert(te_ref, i):
    return (i == 0) | (te_ref[i] != te_ref[jnp.maximum(i - 1, 0)])


def _expert_up_body(layer, te_ref, nx_ref, nu_ref, tok_cur, tok_nxt, h_hbm, w_hbm, b_ref, o_ref,
                    xbuf, wst, wbf, sems, wsem):
    i = pl.program_id(0)
    nt = pl.num_programs(0)
    n_used = nu_ref[0]
    tm = o_ref.shape[0]
    n_chunks = wst.shape[0] // LANES
    slab = tm * TOKEN_PITCH

    def gather_start(tok_ref, slot, r):
        src = pl.multiple_of(tok_ref[0, 0, r] * TOKEN_PITCH, 8)
        dst = pl.multiple_of(slot * slab + r * TOKEN_PITCH, 8)
        pltpu.make_async_copy(h_hbm.at[pl.ds(src, n_chunks)], xbuf.at[pl.ds(dst, n_chunks)], sems.at[slot]).start()

    def gather_wait(slot):
        pltpu.make_async_copy(h_hbm.at[pl.ds(0, tm * n_chunks)], xbuf.at[pl.ds(0, tm * n_chunks)],
                              sems.at[slot]).wait()

    def weight_copy(e):
        return pltpu.make_async_copy(w_hbm.at[layer, e], wst, wsem)

    @pl.when(i == 0)
    def _():
        weight_copy(te_ref[0]).start(priority=WEIGHT_DMA_PRIORITY)

        def body(r, _):
            gather_start(tok_cur, 0, r)
            return 0
        lax.fori_loop(0, tm, body, 0)

    @pl.when(i < n_used)
    def _():
        slot = i % 2
        nxt = (i + 1) % 2
        gather_wait(slot)

        @pl.when(_first_tile_of_expert(te_ref, i))
        def _():
            weight_copy(te_ref[i]).wait()
            wbf[...] = wst[...].astype(BF16)

            @pl.when(nx_ref[i] >= 0)
            def _():
                weight_copy(nx_ref[i]).start(priority=WEIGHT_DMA_PRIORITY)

        base = slot * slab
        x = jnp.concatenate(
            [xbuf[pl.ds(base + c, tm, stride=TOKEN_PITCH), :].astype(BF16) for c in range(n_chunks)], axis=1)
        f = D_EXPERT
        cw = f // UP_CHUNKS
        per = -(-tm // UP_CHUNKS)
        bias = b_ref[0, 0]
        for c in range(UP_CHUNKS):
            for r in range(c * per, min(tm, (c + 1) * per)):
                gather_start(tok_nxt, nxt, r)
            gs = slice(c * cw, (c + 1) * cw)
            us = slice(f + c * cw, f + (c + 1) * cw)
            g = jnp.dot(x, wbf[:, gs], preferred_element_type=F32) + bias[:, gs]
            u = jnp.dot(x, wbf[:, us], preferred_element_type=F32) + bias[:, us]
            gate = jnp.minimum(g, SWIGLU_LIMIT)
            up = jnp.clip(u, -SWIGLU_LIMIT, SWIGLU_LIMIT)
            o_ref[:, gs] = ((up + 1.0) * (gate * jax.nn.sigmoid(SWIGLU_ALPHA * gate))).astype(o_ref.dtype)

        @pl.when(i == nt - 1)
        def _():
            gather_wait(nxt)

    @pl.when(i >= n_used)
    def _():
        @pl.when(i == n_used)
        def _():
            gather_wait(i % 2)

        o_ref[...] = jnp.zeros_like(o_ref)


def expert_up(hs, tok_tiles, tile_expert, next_expert, n_used, w_gu, b_gu, layer, tm=MOE_TM):
    nt = tok_tiles.shape[0]
    d = w_gu.shape[2]
    assert hs.shape[1] == LANES and d // LANES <= TOKEN_PITCH
    f2 = w_gu.shape[3]
    grid_spec = pltpu.PrefetchScalarGridSpec(
        num_scalar_prefetch=3,
        grid=(nt,),
        in_specs=[
            pl.BlockSpec((1, 1, tm), lambda i, *_: (i, 0, 0), memory_space=pltpu.SMEM),
            pl.BlockSpec((1, 1, tm), lambda i, *_: (jnp.minimum(i + 1, nt - 1), 0, 0), memory_space=pltpu.SMEM),
            pl.BlockSpec(memory_space=pl.ANY),
            pl.BlockSpec(memory_space=pl.ANY),
            pl.BlockSpec((1, 1, 1, f2), lambda i, te, nx, nu: (layer, te[i], 0, 0)),
        ],
        out_specs=pl.BlockSpec((tm, f2 // 2), lambda i, *_: (i, 0)),
        scratch_shapes=[
            pltpu.VMEM((2 * tm * TOKEN_PITCH, LANES), F32),
            pltpu.VMEM((d, f2), F32),
            pltpu.VMEM((d, f2), BF16),
            pltpu.SemaphoreType.DMA((2,)),
            pltpu.SemaphoreType.DMA(()),
        ],
    )
    return pl.pallas_call(
        functools.partial(_expert_up_body, layer),
        grid_spec=grid_spec,
        out_shape=jax.ShapeDtypeStruct((nt * tm, f2 // 2), BF16),
        compiler_params=_cparams("arbitrary"),
        name="expert_up",
    )(tile_expert, next_expert, n_used, tok_tiles, tok_tiles, hs, w_gu, b_gu[:, :, None, :])


def _expert_down_body(te_ref, nu_ref, dst_prev, dst_cur, a_ref, w_ref, b_ref, y_hbm, ybuf, wbf, sems):
    i = pl.program_id(0)
    n_used = nu_ref[0]
    tm = a_ref.shape[0]
    d = ybuf.shape[2]

    def scatter_start(dst_ref, slot, r, priority=0):
        _row_copy(ybuf.at[slot], r, y_hbm, dst_ref[0, 0, r], sems.at[slot]).start(priority=priority)

    def scatter_wait(slot):
        pltpu.make_async_copy(ybuf.at[slot], y_hbm.at[pl.ds(0, tm)], sems.at[slot]).wait()

    def compute(slot, prev, scatter_prev):
        a = a_ref[...]
        cw = d // DOWN_CHUNKS
        per = -(-tm // DOWN_CHUNKS)
        for c in range(DOWN_CHUNKS):
            if scatter_prev:
                for r in range(c * per, min(tm, (c + 1) * per)):
                    scatter_start(dst_prev, prev, r, priority=r % 2)
            cs = slice(c * cw, (c + 1) * cw)
            ybuf[slot, :, cs] = jnp.dot(a, wbf[:, cs], preferred_element_type=F32) + b_ref[0, 0][:, cs]

    @pl.when(i < n_used)
    def _():
        slot = i % 2
        prev = (i + 1) % 2

        @pl.when(i >= 2)
        def _():
            scatter_wait(slot)

        @pl.when(_first_tile_of_expert(te_ref, i))
        def _():
            wbf[...] = w_ref[0, 0].astype(BF16)

        @pl.when(i == 0)
        def _():
            compute(slot, prev, False)

        @pl.when(i > 0)
        def _():
            compute(slot, prev, True)

        @pl.when(i == n_used - 1)
        def _():
            def body(r, _):
                scatter_start(dst_cur, slot, r)
                return 0
            lax.fori_loop(0, tm, body, 0)

            @pl.when(i >= 1)
            def _():
                scatter_wait(prev)

            scatter_wait(slot)

    @pl.when(i >= n_used)
    def _():
        @pl.when(i == n_used)
        def _():
            ybuf[0] = jnp.zeros((tm, d), F32)

        zero_fill = pltpu.make_async_copy(ybuf.at[0], y_hbm.at[pl.ds(pl.multiple_of(i * tm, tm), tm)], sems.at[0])
        zero_fill.start()
        zero_fill.wait()


def expert_down(act, dst_tiles, tile_expert, n_used, w_down, b_down, layer, tm=MOE_TM):
    p, f = act.shape
    nt = p // tm
    d = w_down.shape[3]
    grid_spec = pltpu.PrefetchScalarGridSpec(
        num_scalar_prefetch=2,
        grid=(nt,),
        in_specs=[
            pl.BlockSpec((1, 1, tm), lambda i, te, nu: (jnp.maximum(i - 1, 0), 0, 0), memory_space=pltpu.SMEM),
            pl.BlockSpec((1, 1, tm), lambda i, te, nu: (i, 0, 0), memory_space=pltpu.SMEM),
            pl.BlockSpec((tm, f), lambda i, te, nu: (i, 0)),
            pl.BlockSpec((1, 1, f, d), lambda i, te, nu: (layer, te[i], 0, 0)),
            pl.BlockSpec((1, 1, 1, d), lambda i, te, nu: (layer, te[i], 0, 0)),
        ],
        out_specs=pl.BlockSpec(memory_space=pl.ANY),
        scratch_shapes=[pltpu.VMEM((2, tm, d), F32), pltpu.VMEM((f, d), BF16), pltpu.SemaphoreType.DMA((2,))],
    )
    return pl.pallas_call(
        _expert_down_body,
        grid_spec=grid_spec,
        out_shape=jax.ShapeDtypeStruct((p, d), F32),
        compiler_params=_cparams("arbitrary"),
        name="expert_down",
    )(tile_expert, n_used, dst_tiles, dst_tiles, act, w_down, b_down[:, :, None, :])


def _combine_body(wgt_ref, x_ref, gate_ref, y0_ref, y1_ref, y2_ref, y3_ref, o_ref):
    w = wgt_ref[...]
    f = w[:, 0:1] * y0_ref[...]
    for k, y_ref in enumerate((y1_ref, y2_ref, y3_ref), start=1):
        f = f + w[:, k:k + 1] * y_ref[...]
    o_ref[...] = x_ref[...] + gate_ref[0] * f


def combine(x_all, y_planes, wgt, gate_tbl, split_tile, tb=COMBINE_TB):
    t, d = x_all.shape
    nsteps = t // tb
    if split_tile is None:
        sel = lambda i: (0, 0, 0)
    else:
        sel = lambda i: (jnp.where(i >= split_tile, 1, 0), 0, 0)
    plane = lambda k: pl.BlockSpec((tb, d), lambda i: (k * nsteps + i, 0))
    return pl.pallas_call(
        _combine_body,
        grid=(nsteps,),
        in_specs=[
            pl.BlockSpec((tb, TOP_K), lambda i: (i, 0)),
            pl.BlockSpec((tb, d), lambda i: (i, 0)),
            pl.BlockSpec((1, 1, d), sel),
        ] + [plane(k) for k in range(TOP_K)],
        out_specs=pl.BlockSpec((tb, d), lambda i: (i, 0)),
        out_shape=jax.ShapeDtypeStruct((t, d), F32),
        compiler_params=_cparams("parallel"),
        name="moe_combine",
    )(wgt, x_all, gate_tbl, *([y_planes] * TOP_K))


def _routing_tables(idx4, rank4, counts, t, tm, nt):
    i32 = jnp.int32
    e_ids = jnp.arange(N_EXPERTS, dtype=i32)
    cnt = counts[0].astype(i32)
    tiles_e = (cnt + tm - 1) // tm
    tile_end = jnp.cumsum(tiles_e)
    offsets = (tile_end - tiles_e) * tm
    n_used = tile_end[-1]
    slot4 = jnp.sum(jnp.where(idx4[:, :, None] == e_ids, offsets, 0), axis=-1) + rank4
    te = jnp.sum((tile_end[None, :] <= jnp.arange(nt, dtype=i32)[:, None]).astype(i32), axis=1)
    last_e = jnp.sum(jnp.where(jnp.arange(nt, dtype=i32) == n_used - 1, te, 0))
    te = jnp.minimum(te, last_e)
    later_used = (tiles_e[None, :] > 0) & (e_ids[None, :] > e_ids[:, None])
    nxt_e = jnp.min(jnp.where(later_used, e_ids[None, :], N_EXPERTS), axis=1)
    nxt_e = jnp.where(nxt_e >= N_EXPERTS, -1, nxt_e)
    nx = jnp.sum(jnp.where(te[:, None] == e_ids, nxt_e, 0), axis=1).astype(i32)
    assign = jnp.arange(t * TOP_K, dtype=i32)
    code = jnp.full((nt * tm,), -1, i32).at[slot4.reshape(-1)].set(assign)
    is_pad = code < 0
    tok = jnp.where(is_pad, 0, code // TOP_K)
    pad_row = TOP_K * t + jnp.cumsum(is_pad.astype(i32)) - 1
    dst = jnp.where(is_pad, pad_row, (code % TOP_K) * t + code // TOP_K)
    return tok, dst, te.astype(i32), nx, n_used.astype(i32).reshape(1)


def moe_block(x_all, g_ffn, scale_tbl, shift_tbl, gate_tbl, split_rows, w_router, b_router, w_gu, b_gu,
              w_down, b_down, layer):
    t, d = x_all.shape
    tm = MOE_TM
    hs, idx4, wgt4, rank4, counts = router(x_all, g_ffn, scale_tbl, shift_tbl,
                                           None if split_rows is None else split_rows // ROW_TILE,
                                           w_router, b_router, layer)
    nt = (t * TOP_K + N_EXPERTS * (tm - 1) + tm - 1) // tm
    tok, dst, te, nx, n_used = _routing_tables(idx4, rank4, counts, t, tm, nt)
    act = expert_up(hs, tok.reshape(nt, 1, tm), te, nx, n_used, w_gu, b_gu, layer)
    y_planes = expert_down(act, dst.reshape(nt, 1, tm), te, n_used, w_down, b_down, layer)
    return combine(x_all, y_planes, wgt4, gate_tbl, None if split_rows is None else split_rows // COMBINE_TB)


def _rope_angles(n_tokens, rot_dim):
    rows = n_tokens // GRID_W
    row = jnp.repeat(jnp.arange(rows, dtype=F32), GRID_W)
    col = jnp.tile(jnp.arange(GRID_W, dtype=F32), rows)
    n_freq = rot_dim // 4
    inv_freq = ROPE_THETA ** (-jnp.arange(n_freq, dtype=F32) / n_freq)
    ang = jnp.concatenate([row[:, None] * inv_freq, col[:, None] * inv_freq], axis=-1)
    return jnp.cos(ang), jnp.sin(ang)


def _rope_tables(n, n_ctx):
    cos, sin = _rope_angles(n, 64)
    z = jnp.zeros_like(cos)
    pair = (jnp.concatenate([cos] * 4, axis=1), jnp.concatenate([-sin, sin, -sin, sin], axis=1))
    single = (jnp.concatenate([cos, cos, z, z], axis=1), jnp.concatenate([-sin, sin, z, z], axis=1))
    ident = (jnp.ones((n_ctx, LANES), F32), jnp.zeros((n_ctx, LANES), F32))
    return pair, single, ident


def _pad_heads(w, n_heads, widths, pad_to):
    k = w.shape[0]
    per = sum(widths)
    w3 = w.reshape(k, n_heads, per)[:, :, :widths[0]]
    return jnp.pad(w3, ((0, 0), (0, 0), (0, pad_to - widths[0]))).reshape(k, n_heads * pad_to)


def _kv_weight(w_ukv_l):
    kin = C_KV_RANK + 2 * C_ROPE
    w_kv3 = w_ukv_l.reshape(C_KV_RANK, C_HEADS, C_NOPE + C_HEAD_V)
    w_k = jnp.pad(w_kv3[:, :, :C_NOPE], ((0, 0), (0, 0), (0, C_QK_PAD - C_NOPE)))
    rope_rows = jnp.pad(jnp.eye(C_ROPE, dtype=F32), ((0, 0), (C_NOPE, C_QK_PAD - C_NOPE - C_ROPE)))
    rope_rows = jnp.broadcast_to(rope_rows[:, None, :], (C_ROPE, C_HEADS, C_QK_PAD))
    w_k = jnp.concatenate([w_k, rope_rows, jnp.zeros((C_ROPE, C_HEADS, C_QK_PAD), F32)], axis=0)
    w_v = jnp.pad(w_kv3[:, :, C_NOPE:], ((0, 2 * C_ROPE), (0, 0), (0, 0)))
    return jnp.concatenate([w_k.reshape(kin, -1), w_v.reshape(kin, -1)], axis=1).astype(BF16)


def _tbl(lat_row, ctx_row=None):
    rows = [lat_row] if ctx_row is None else [lat_row, ctx_row]
    return jnp.stack(rows)[:, None, :]


def kernel(x, c, ctx, c_ctx, w_mod, b_mod, g_mix, g_ffn, w_in_ab, ws_a, bs_a, sink_b, w_out_ab, w_in_cd, g_cq,
           w_uq, g_ckv, w_ukv, w_out_cd, w_router, b_router, w_gu, b_gu, w_down, b_down, g_final):
    bsz, n, d = x.shape
    n_ctx = ctx.shape[1]
    assert bsz == 1 and d == D_MODEL and n == FFT_A * FFT_B
    xl = x[0]
    xc = ctx[0]
    c_rows = jnp.zeros((8, d), F32).at[0].set(c[0]).at[1].set(c_ctx)
    pair_t, single_t, ident_t = _rope_tables(n, n_ctx)
    zeros_tbl = jnp.zeros((1, 1, d), F32)

    def mod_rows(l):
        mod = modulation(c_rows, w_mod, b_mod, l)
        lat = [mod[0, k * d:(k + 1) * d] for k in range(N_MOD)]
        cx = [mod[1, k * d:(k + 1) * d] for k in range(N_MOD)]
        return lat, cx

    lat, cx = mod_rows(0)
    h = norm_mod(xl, g_mix[0], _tbl(lat[1]), _tbl(lat[0]), out_dtype=BF16, name="norm_mix0")
    hc = norm_mod(xc, g_mix[0], _tbl(cx[1]), _tbl(cx[0]), out_dtype=BF16, name="norm_mix0_ctx")
    w_in = w_in_ab[0].astype(BF16)
    z = matmul([h], [w_in], out_dtype=BF16, tm=1024, tn=512, a_resident=True, name="proj_ab")
    zc = matmul([hc], [w_in], out_dtype=BF16, tm=n_ctx, tn=512, a_resident=True, name="proj_ab_ctx")
    ws_bf = ws_a[0].astype(BF16)
    bs_col = bs_a[0][:, :, None]
    a_lat = gmlp(z, ws_bf, bs_col)
    a_ctx = gmlp(zc, ws_bf, bs_col)
    q_lat = rope_q(z, *pair_t)
    q_ctx = rope_q(zc, *ident_t)
    kd, vd = rope_kv(z, *pair_t)
    kd_c, vd_c = rope_kv(zc, *ident_t)
    sink4 = sink_b[0].astype(F32).reshape(B_KV_HEADS, B_GROUP, 1, 1)
    b_lat = window_attn(q_lat, kd, vd, kd_c, vd_c, sink4, True)
    b_ctx = window_attn(q_ctx, None, None, kd_c, vd_c, sink4, False)
    w_out = w_out_ab[0].astype(BF16)
    xl = matmul([a_lat, b_lat], [w_out[:A_W], w_out[A_W:]], out_dtype=F32, tm=512, tn=1024,
                res=xl, gate=lat[2][None], name="out_ab")
    xc = matmul([a_ctx, b_ctx], [w_out[:A_W], w_out[A_W:]], out_dtype=F32, tm=n_ctx, tn=1024,
                res=xc, gate=cx[2][None], name="out_ab_ctx")
    x_all = jnp.concatenate([xl, xc], axis=0)
    x_all = moe_block(x_all, g_ffn[0], _tbl(lat[4], cx[4]), _tbl(lat[3], cx[3]), _tbl(lat[5], cx[5]), n,
                      w_router, b_router, w_gu, b_gu, w_down, b_down, 0)
    xl = x_all
    xc = x_all[n:]

    lat, cx = mod_rows(1)
    h = norm_mod(xl, g_mix[1], _tbl(lat[1]), _tbl(lat[0]), out_dtype=BF16, rows=n, name="norm_mix1")
    hc = norm_mod(xc, g_mix[1], _tbl(cx[1]), _tbl(cx[0]), out_dtype=BF16, name="norm_mix1_ctx")
    w_in = w_in_cd[0]
    w_cq = w_in[:, :CD_KV0].astype(BF16)
    w_kvr = jnp.pad(w_in[:, CD_KV0:CD_D0], ((0, 0), (0, 64))).astype(BF16)
    w_dz = w_in[:, CD_D0:].astype(BF16)
    cq = matmul([h], [w_cq], out_dtype=BF16, tm=1024, tn=C_Q_RANK, name="proj_cq")
    ckvkr = matmul([h], [w_kvr], out_dtype=BF16, tm=1024, tn=640, name="proj_ckv")
    ckvkr_c = matmul([hc], [w_kvr], out_dtype=BF16, tm=n_ctx, tn=640, name="proj_ckv_ctx")
    dz = matmul([h], [w_dz], out_dtype=F32, tm=1024, tn=D_W, name="proj_dz")
    cqn = norm_mod(cq, g_cq[0], jnp.zeros((1, 1, C_Q_RANK), F32), jnp.zeros((1, 1, C_Q_RANK), F32),
                   out_dtype=BF16, name="norm_cq")
    w_uq_p = _pad_heads(w_uq[0], C_HEADS, (C_NOPE + C_ROPE,), C_QK_PAD).astype(BF16)
    qp = matmul([cqn], [w_uq_p], out_dtype=BF16, tm=1024, tn=1024, name="proj_uq")
    kv_in = kvprep(ckvkr, g_ckv[0], *single_t)
    kv_in_c = kvprep(ckvkr_c, g_ckv[0], *ident_t)
    w_kv = _kv_weight(w_ukv[0])
    kv = matmul([kv_in], [w_kv], out_dtype=BF16, tm=1024, tn=1024, name="proj_ukv")
    kv_c = matmul([kv_in_c], [w_kv], out_dtype=BF16, tm=n_ctx, tn=1024, name="proj_ukv_ctx")
    c_lat = mla_attn(qp, kv, kv_c, *single_t)
    d_lat = fourier_mix(dz)
    w_out = w_out_cd[0].astype(BF16)
    n_c = C_HEADS * C_HEAD_V
    xl = matmul([c_lat, d_lat], [w_out[:n_c], w_out[n_c:]], out_dtype=F32, tm=512, tn=1024,
                res=xl, gate=lat[2][None], name="out_cd")
    xl = moe_block(xl, g_ffn[1], _tbl(lat[4]), _tbl(lat[3]), _tbl(lat[5]), None,
                   w_router, b_router, w_gu, b_gu, w_down, b_down, 1)
    out = norm_mod(xl, g_final, zeros_tbl, zeros_tbl, out_dtype=F32, name="norm_final")
    return out[None]
```

```python
import functools
import math

import jax
import jax.numpy as jnp
from jax import lax
from jax.experimental import pallas as pl
from jax.experimental.pallas import tpu as pltpu

F32 = jnp.float32
BF16 = jnp.bfloat16

D_MODEL = 4096
GRID_W = 64
BLOCK = 128
CHUNK = 128
ROPE_THETA = 10000.0
EPS = 1e-6
NEG_INF = -1e30
N_MOD = 6
A_GROUPS = 8
A_GROUP_DIM = D_MODEL // 16
A_W = A_GROUPS * A_GROUP_DIM
B_HEAD_DIM = 64
B_HEADS = (D_MODEL // 2) // B_HEAD_DIM
B_KV_HEADS = B_HEADS // 8
B_GROUP = B_HEADS // B_KV_HEADS
B_QW = B_HEADS * B_HEAD_DIM
B_KW = B_KV_HEADS * B_HEAD_DIM
B_SCALE = B_HEAD_DIM ** -0.5
AB_KV0 = 2 * A_W + B_QW
C_HEAD_V = 128
C_HEADS = (3 * D_MODEL // 4) // C_HEAD_V
C_Q_RANK = 3 * D_MODEL // 16
C_KV_RANK = D_MODEL // 8
C_NOPE = 128
C_ROPE = 64
C_SCALE = (C_NOPE + C_ROPE) ** -0.5
C_QK_PAD = 256
D_GROUPS = 4
D_GROUP_DIM = D_MODEL // 16
D_W = D_GROUPS * D_GROUP_DIM
CD_KV0 = C_Q_RANK
CD_D0 = C_Q_RANK + C_KV_RANK + C_ROPE
N_EXPERTS = 32
TOP_K = 4
D_EXPERT = 3 * D_MODEL // 16
SWIGLU_LIMIT = 7.0
SWIGLU_ALPHA = 1.702
LOG2_E = 1.4426950408889634

V7X_VMEM_BYTES = 64 * 1024 * 1024
VMEM_LIMIT_BYTES = V7X_VMEM_BYTES - 8 * 1024 * 1024
LANES = 128

ROW_TILE = 256
MOE_TM = 256
COMBINE_TB = 128
UP_CHUNKS = 3
DOWN_CHUNKS = 4
WEIGHT_DMA_PRIORITY = 1
TOKEN_PITCH = 40
MLA_TQ = 1024
MLA_TK = 1024
FFT_A = 64
FFT_B = 128


def _cparams(*sem):
    return pltpu.CompilerParams(dimension_semantics=sem, vmem_limit_bytes=VMEM_LIMIT_BYTES)


def _mm_body(n_a, has_bias, has_res, *refs):
    a_refs = refs[:n_a]
    w_refs = refs[n_a:2 * n_a]
    pos = 2 * n_a
    acc = None
    for a_ref, w_ref in zip(a_refs, w_refs):
        d = jnp.dot(a_ref[...], w_ref[...], preferred_element_type=F32)
        acc = d if acc is None else acc + d
    if has_bias:
        acc = acc + refs[pos][...]
        pos += 1
    if has_res:
        acc = refs[pos][...] + refs[pos + 1][...] * acc
    o_ref = refs[-1]
    o_ref[...] = acc.astype(o_ref.dtype)


def matmul(a_list, w_list, *, out_dtype, tm, tn, bias=None, res=None, gate=None, a_resident=False, name="mm"):
    m = a_list[0].shape[0]
    n = w_list[0].shape[1]
    assert m % tm == 0 and n % tn == 0, (m, tm, n, tn)
    if a_resident:
        grid = (m // tm, n // tn)
        im = lambda i, j: i
        jn = lambda i, j: j
    else:
        grid = (n // tn, m // tm)
        im = lambda j, i: i
        jn = lambda j, i: j
    in_specs = []
    for a in a_list:
        in_specs.append(pl.BlockSpec((tm, a.shape[1]), lambda *g: (im(*g), 0)))
    for w in w_list:
        in_specs.append(pl.BlockSpec((w.shape[0], tn), lambda *g: (0, jn(*g))))
    args = list(a_list) + list(w_list)
    if bias is not None:
        in_specs.append(pl.BlockSpec((1, tn), lambda *g: (0, jn(*g))))
        args.append(bias)
    if res is not None:
        in_specs.append(pl.BlockSpec((tm, tn), lambda *g: (im(*g), jn(*g))))
        in_specs.append(pl.BlockSpec((1, tn), lambda *g: (0, jn(*g))))
        args += [res, gate]
    return pl.pallas_call(
        functools.partial(_mm_body, len(a_list), bias is not None, res is not None),
        grid=grid,
        in_specs=in_specs,
        out_specs=pl.BlockSpec((tm, tn), lambda *g: (im(*g), jn(*g))),
        out_shape=jax.ShapeDtypeStruct((m, n), out_dtype),
        compiler_params=_cparams("parallel", "parallel"),
        name=name,
    )(*args)


def _mod_body(c_ref, w_ref, b_ref, o_ref):
    c = c_ref[...]
    a = (c * jax.nn.sigmoid(c)).astype(BF16)
    o_ref[...] = jnp.dot(a, w_ref[0].astype(BF16), preferred_element_type=F32) + b_ref[0]


def modulation(c_rows, w_mod, b_mod, layer, tn=1024):
    _, d, n = w_mod.shape
    return pl.pallas_call(
        _mod_body,
        grid=(n // tn,),
        in_specs=[
            pl.BlockSpec((8, d), lambda j: (0, 0)),
            pl.BlockSpec((1, d, tn), lambda j: (layer, 0, j)),
            pl.BlockSpec((1, 1, tn), lambda j: (layer, 0, j)),
        ],
        out_specs=pl.BlockSpec((8, tn), lambda j: (0, j)),
        out_shape=jax.ShapeDtypeStruct((8, n), F32),
        compiler_params=_cparams("parallel"),
        name="modulation",
    )(c_rows, w_mod, b_mod[:, None, :])


def _norm_mod_body(x_ref, g_ref, scale_ref, shift_ref, o_ref):
    x = x_ref[...].astype(F32)
    y = x * lax.rsqrt(jnp.mean(x * x, axis=-1, keepdims=True) + EPS)
    y = y * g_ref[...]
    o_ref[...] = (y * (1.0 + scale_ref[0]) + shift_ref[0]).astype(o_ref.dtype)


def norm_mod(x, g, scale_tbl, shift_tbl, *, out_dtype, rows=None, split_tile=None, tm=ROW_TILE, name="norm_mod"):
    d = x.shape[1]
    rows = x.shape[0] if rows is None else rows
    assert rows % tm == 0
    if split_tile is None:
        sel = lambda i: (0, 0, 0)
    else:
        sel = lambda i: (jnp.where(i >= split_tile, 1, 0), 0, 0)
    return pl.pallas_call(
        _norm_mod_body,
        grid=(rows // tm,),
        in_specs=[
            pl.BlockSpec((tm, d), lambda i: (i, 0)),
            pl.BlockSpec((1, d), lambda i: (0, 0)),
            pl.BlockSpec((1, 1, d), sel),
            pl.BlockSpec((1, 1, d), sel),
        ],
        out_specs=pl.BlockSpec((tm, d), lambda i: (i, 0)),
        out_shape=jax.ShapeDtypeStruct((rows, d), out_dtype),
        compiler_params=_cparams("parallel"),
        name=name,
    )(x, g[None].astype(F32), scale_tbl, shift_tbl)


def _gmlp_body(u_ref, v_ref, ws_ref, bs_ref, o_ref):
    for g in range(A_GROUPS):
        sl = slice(g * A_GROUP_DIM, (g + 1) * A_GROUP_DIM)
        ug = jax.nn.gelu(u_ref[:, sl].astype(F32))
        vf = jax.nn.gelu(v_ref[:, sl].astype(F32))
        mu = jnp.mean(vf, axis=-1, keepdims=True)
        vc = vf - mu
        var = jnp.mean(vc * vc, axis=-1, keepdims=True)
        vn = (vc * lax.rsqrt(var + EPS)).astype(BF16)
        mixed = jnp.dot(ws_ref[g], vn, preferred_element_type=F32) + bs_ref[g]
        o_ref[:, sl] = (ug * mixed).astype(o_ref.dtype)


def gmlp(z, ws_bf, bs_col):
    n = z.shape[0]
    return pl.pallas_call(
        _gmlp_body,
        grid=(n // CHUNK,),
        in_specs=[
            pl.BlockSpec((CHUNK, A_W), lambda i: (i, 0)),
            pl.BlockSpec((CHUNK, A_W), lambda i: (i, 1)),
            pl.BlockSpec((A_GROUPS, CHUNK, CHUNK), lambda i: (0, 0, 0)),
            pl.BlockSpec((A_GROUPS, CHUNK, 1), lambda i: (0, 0, 0)),
        ],
        out_specs=pl.BlockSpec((CHUNK, A_W), lambda i: (i, 0)),
        out_shape=jax.ShapeDtypeStruct((n, A_W), BF16),
        compiler_params=_cparams("parallel"),
        name="gmlp",
    )(z, z, ws_bf, bs_col)


def _swap_halves(x):
    lane = lax.broadcasted_iota(jnp.int32, x.shape, 1)
    return jnp.where((lane & 63) < 32, pltpu.roll(x, 96, 1), pltpu.roll(x, 32, 1))


def _rope_slab(x, cos, sin):
    return x * cos + _swap_halves(x) * sin


def _rope_q_body(z_ref, cos_ref, sin_ref, o_ref):
    cos = cos_ref[...]
    sin = sin_ref[...]
    for s in range(z_ref.shape[1] // LANES):
        sl = slice(s * LANES, (s + 1) * LANES)
        x = z_ref[:, sl].astype(F32)
        o_ref[:, sl] = (_rope_slab(x, cos, sin) * B_SCALE).astype(o_ref.dtype)


def rope_q(z, cos_t, sin_t, tm=ROW_TILE):
    n = z.shape[0]
    col0 = (2 * A_W) // B_QW
    return pl.pallas_call(
        _rope_q_body,
        grid=(n // tm,),
        in_specs=[
            pl.BlockSpec((tm, B_QW), lambda i: (i, col0)),
            pl.BlockSpec((tm, LANES), lambda i: (i, 0)),
            pl.BlockSpec((tm, LANES), lambda i: (i, 0)),
        ],
        out_specs=pl.BlockSpec((tm, B_QW), lambda i: (i, 0)),
        out_shape=jax.ShapeDtypeStruct((n, B_QW), BF16),
        compiler_params=_cparams("parallel"),
        name="rope_q",
    )(z, cos_t, sin_t)


def _dup_heads(x):
    lane = lax.broadcasted_iota(jnp.int32, x.shape, 1)
    r = pltpu.roll(x, 64, 1)
    lo = lane < 64
    return jnp.where(lo, x, r), jnp.where(lo, r, x)


def _rope_kv_body(z_ref, cos_ref, sin_ref, kd_ref, vd_ref):
    cos = cos_ref[...]
    sin = sin_ref[...]
    for s in range(B_KW // LANES):
        xk = z_ref[:, s * LANES:(s + 1) * LANES].astype(F32)
        e, o = _dup_heads(_rope_slab(xk, cos, sin))
        kd_ref[:, (2 * s) * LANES:(2 * s + 1) * LANES] = e.astype(kd_ref.dtype)
        kd_ref[:, (2 * s + 1) * LANES:(2 * s + 2) * LANES] = o.astype(kd_ref.dtype)
        xv = z_ref[:, B_KW + s * LANES:B_KW + (s + 1) * LANES].astype(F32)
        e, o = _dup_heads(xv)
        vd_ref[:, (2 * s) * LANES:(2 * s + 1) * LANES] = e.astype(vd_ref.dtype)
        vd_ref[:, (2 * s + 1) * LANES:(2 * s + 2) * LANES] = o.astype(vd_ref.dtype)


def rope_kv(z, cos_t, sin_t, tm=ROW_TILE):
    n = z.shape[0]
    wb = 2 * B_KW
    col0 = AB_KV0 // wb
    out_w = B_KV_HEADS * LANES
    return pl.pallas_call(
        _rope_kv_body,
        grid=(n // tm,),
        in_specs=[
            pl.BlockSpec((tm, wb), lambda i: (i, col0)),
            pl.BlockSpec((tm, LANES), lambda i: (i, 0)),
            pl.BlockSpec((tm, LANES), lambda i: (i, 0)),
        ],
        out_specs=[pl.BlockSpec((tm, out_w), lambda i: (i, 0)), pl.BlockSpec((tm, out_w), lambda i: (i, 0))],
        out_shape=[jax.ShapeDtypeStruct((n, out_w), BF16), jax.ShapeDtypeStruct((n, out_w), BF16)],
        compiler_params=_cparams("parallel"),
        name="rope_kv",
    )(z, cos_t, sin_t)


def _window_attn_body(has_window, nb, *refs):
    if has_window:
        q_ref, kp_ref, kc_ref, kn_ref, vp_ref, vc_ref, vn_ref, kx_ref, vx_ref, sink_ref, o_ref = refs
    else:
        q_ref, kx_ref, vx_ref, sink_ref, o_ref = refs
    i = pl.program_id(0)
    n_ctx = kx_ref.shape[0]
    lane = lax.broadcasted_iota(jnp.int32, (BLOCK, LANES), 1)
    lo = lane < 64
    if has_window:
        r = lax.broadcasted_iota(jnp.int32, (BLOCK, BLOCK), 0)
        c = lax.broadcasted_iota(jnp.int32, (BLOCK, BLOCK), 1)
        ones = jnp.ones((BLOCK, BLOCK), jnp.bool_)
        valid = jnp.concatenate(
            [(c >= r) & (i > 0), ones, (c <= r) & (i < nb - 1), jnp.ones((BLOCK, n_ctx), jnp.bool_)], axis=1)
    for j in range(B_KV_HEADS):
        hs = slice(j * LANES, (j + 1) * LANES)
        if has_window:
            kj = jnp.concatenate([kp_ref[:, hs], kc_ref[:, hs], kn_ref[:, hs], kx_ref[:, hs]], axis=0)
            vj = jnp.concatenate([vp_ref[:, hs], vc_ref[:, hs], vn_ref[:, hs], vx_ref[:, hs]], axis=0)
        else:
            kj = kx_ref[:, hs]
            vj = vx_ref[:, hs]
        rows = []
        for m in range(B_GROUP // 2):
            slab = q_ref[:, (j * (B_GROUP // 2) + m) * LANES:(j * (B_GROUP // 2) + m + 1) * LANES]
            zero = jnp.zeros_like(slab)
            rows += [jnp.where(lo, slab, zero), jnp.where(lo, zero, slab)]
        qs = jnp.concatenate(rows, axis=0)
        s = lax.dot_general(qs, kj, (((1,), (1,)), ((), ())), preferred_element_type=F32)
        s = s.reshape(B_GROUP, BLOCK, kj.shape[0])
        if has_window:
            s = jnp.where(valid[None], s, NEG_INF)
        sink = sink_ref[j]
        mx = jnp.maximum(jnp.max(s, axis=-1, keepdims=True), sink)
        p = jnp.exp(s - mx)
        denom = jnp.sum(p, axis=-1, keepdims=True) + jnp.exp(sink - mx)
        o = jnp.dot(p.reshape(B_GROUP * BLOCK, kj.shape[0]).astype(BF16), vj, preferred_element_type=F32)
        o = o.reshape(B_GROUP, BLOCK, LANES) / denom
        for m in range(B_GROUP // 2):
            col = (j * (B_GROUP // 2) + m) * LANES
            o_ref[:, col:col + LANES] = jnp.where(lo, o[2 * m], o[2 * m + 1]).astype(o_ref.dtype)


def window_attn(q, kd, vd, kd_ctx, vd_ctx, sink4, has_window):
    n = q.shape[0]
    nb = n // BLOCK
    n_ctx = kd_ctx.shape[0]
    kw = B_KV_HEADS * LANES
    blk = lambda f: pl.BlockSpec((BLOCK, kw), f)
    full_ctx = pl.BlockSpec((n_ctx, kw), lambda i: (0, 0))
    in_specs = [pl.BlockSpec((BLOCK, B_QW), lambda i: (i, 0))]
    args = [q]
    if has_window:
        prev = lambda i: (jnp.maximum(i - 1, 0), 0)
        cur = lambda i: (i, 0)
        nxt = lambda i: (jnp.minimum(i + 1, nb - 1), 0)
        in_specs += [blk(prev), blk(cur), blk(nxt), blk(prev), blk(cur), blk(nxt)]
        args += [kd, kd, kd, vd, vd, vd]
    in_specs += [full_ctx, full_ctx, pl.BlockSpec((B_KV_HEADS, B_GROUP, 1, 1), lambda i: (0, 0, 0, 0))]
    args += [kd_ctx, vd_ctx, sink4]
    return pl.pallas_call(
        functools.partial(_window_attn_body, has_window, nb),
        grid=(nb,),
        in_specs=in_specs,
        out_specs=pl.BlockSpec((BLOCK, B_QW), lambda i: (i, 0)),
        out_shape=jax.ShapeDtypeStruct((n, B_QW), BF16),
        compiler_params=_cparams("parallel"),
        name="window_attn" if has_window else "ctx_attn",
    )(*args)


def _mla_rope(u, cos, sin):
    return u * cos + _swap_halves(u) * sin


def _kvprep_body(z_ref, g_ref, cos_ref, sin_ref, o_ref):
    ckv = z_ref[:, :C_KV_RANK].astype(F32)
    y = ckv * lax.rsqrt(jnp.mean(ckv * ckv, axis=-1, keepdims=True) + EPS) * g_ref[...]
    o_ref[:, :C_KV_RANK] = y.astype(o_ref.dtype)
    u = z_ref[:, C_KV_RANK:].astype(F32)
    o_ref[:, C_KV_RANK:] = _mla_rope(u, cos_ref[...], sin_ref[...]).astype(o_ref.dtype)


def kvprep(ckvkr, g_ckv, cos_t, sin_t, tm=ROW_TILE):
    n, w = ckvkr.shape
    return pl.pallas_call(
        _kvprep_body,
        grid=(n // tm,),
        in_specs=[
            pl.BlockSpec((tm, w), lambda i: (i, 0)),
            pl.BlockSpec((1, C_KV_RANK), lambda i: (0, 0)),
            pl.BlockSpec((tm, LANES), lambda i: (i, 0)),
            pl.BlockSpec((tm, LANES), lambda i: (i, 0)),
        ],
        out_specs=pl.BlockSpec((tm, w), lambda i: (i, 0)),
        out_shape=jax.ShapeDtypeStruct((n, w), BF16),
        compiler_params=_cparams("parallel"),
        name="kvprep",
    )(ckvkr, g_ckv[None].astype(F32), cos_t, sin_t)


def _mla_body(tk, q_ref, cos_ref, sin_ref, k_ref, v_ref, kx_ref, vx_ref, o_ref):
    tq = q_ref.shape[0]
    sc = C_SCALE * LOG2_E
    qn = q_ref[:, :LANES].astype(F32) * sc
    qr = _mla_rope(q_ref[:, LANES:].astype(F32), cos_ref[...], sin_ref[...]) * sc
    q = jnp.concatenate([qn, qr], axis=1).astype(BF16)

    def ones_column(rows):
        lane = lax.broadcasted_iota(jnp.int32, (rows, LANES), 1)
        return jnp.where(lane == 0, 1.0, 0.0).astype(BF16)

    def step(k, v, carry):
        m, acc = carry
        s = lax.dot_general(q, k, (((1,), (1,)), ((), ())), preferred_element_type=F32)
        m_new = jnp.maximum(m, jnp.max(s, axis=-1, keepdims=True))
        alpha = jnp.exp2(m - m_new)
        p = jnp.exp2(s - m_new).astype(BF16)
        v1 = jnp.concatenate([v, ones_column(v.shape[0])], axis=1)
        return m_new, alpha * acc + jnp.dot(p, v1, preferred_element_type=F32)

    carry = (jnp.full((tq, 1), -jnp.inf, F32), jnp.zeros((tq, C_HEAD_V + LANES), F32))
    for c in range(k_ref.shape[0] // tk):
        carry = step(k_ref[c * tk:(c + 1) * tk, :], v_ref[c * tk:(c + 1) * tk, :], carry)
    m, acc = step(kx_ref[...], vx_ref[...], carry)
    o_ref[...] = (acc[:, :C_HEAD_V] / acc[:, C_HEAD_V:C_HEAD_V + 1]).astype(o_ref.dtype)


def mla_attn(qp, kv, kv_ctx, cos_t, sin_t, tq=MLA_TQ, tk=MLA_TK):
    n = qp.shape[0]
    n_ctx = kv_ctx.shape[0]
    v_col0 = (C_HEADS * C_QK_PAD) // C_HEAD_V
    return pl.pallas_call(
        functools.partial(_mla_body, tk),
        grid=(C_HEADS, n // tq),
        in_specs=[
            pl.BlockSpec((tq, C_QK_PAD), lambda h, i: (i, h)),
            pl.BlockSpec((tq, LANES), lambda h, i: (i, 0)),
            pl.BlockSpec((tq, LANES), lambda h, i: (i, 0)),
            pl.BlockSpec((n, C_QK_PAD), lambda h, i: (0, h)),
            pl.BlockSpec((n, C_HEAD_V), lambda h, i: (0, v_col0 + h)),
            pl.BlockSpec((n_ctx, C_QK_PAD), lambda h, i: (0, h)),
            pl.BlockSpec((n_ctx, C_HEAD_V), lambda h, i: (0, v_col0 + h)),
        ],
        out_specs=pl.BlockSpec((tq, C_HEAD_V), lambda h, i: (i, h)),
        out_shape=jax.ShapeDtypeStruct((n, C_HEADS * C_HEAD_V), BF16),
        compiler_params=_cparams("parallel", "parallel"),
        name="mla_attn",
    )(qp, cos_t, sin_t, kv, kv, kv_ctx, kv_ctx)


_HI = lax.Precision.HIGHEST


def _fft1_body(x_ref, cs_ref, tr_ref, ti_ref, o_ref):
    u = jnp.dot(cs_ref[...], x_ref[...], precision=_HI, preferred_element_type=F32)
    ur, ui = u[:FFT_A], u[FFT_A:]
    reps = x_ref.shape[1] // LANES
    tr = jnp.concatenate([tr_ref[0]] * reps, axis=1)
    ti = jnp.concatenate([ti_ref[0]] * reps, axis=1)
    o_ref[0] = tr * ur - ti * ui
    o_ref[1] = tr * ui + ti * ur


def _fft2_body(v_ref, w2_ref, cc_ref, sc_ref, o_ref):
    vv = jnp.concatenate([v_ref[0, 0], v_ref[1, 0]], axis=0)
    g = jnp.dot(w2_ref[...], vv, precision=_HI, preferred_element_type=F32)
    gr, gi = g[:FFT_B], g[FFT_B:]
    for grp in range(D_GROUPS):
        sl = slice(grp * D_GROUP_DIM, (grp + 1) * D_GROUP_DIM)
        y = (jnp.dot(gr[:, sl], cc_ref[...], precision=_HI, preferred_element_type=F32)
             + jnp.dot(gi[:, sl], sc_ref[...], precision=_HI, preferred_element_type=F32))
        o_ref[:, sl] = y.astype(o_ref.dtype)


def _dft_tables(n):
    a_n, b_n = FFT_A, FFT_B
    assert a_n * b_n == n
    two_pi = 2.0 * math.pi

    def ang(i, j, period):
        return (two_pi / period) * ((i[:, None] * j[None, :]) % period).astype(F32)

    ia = jnp.arange(a_n, dtype=jnp.int32)
    ib = jnp.arange(b_n, dtype=jnp.int32)
    ic = jnp.arange(D_GROUP_DIM, dtype=jnp.int32)
    th_a = ang(ia, ia, a_n)
    cs = jnp.concatenate([jnp.cos(th_a), -jnp.sin(th_a)], axis=0)
    th_t = ang(ib, ia, n)
    tr = jnp.broadcast_to(jnp.cos(th_t)[:, :, None], (b_n, a_n, LANES))
    ti = jnp.broadcast_to(-jnp.sin(th_t)[:, :, None], (b_n, a_n, LANES))
    th_b = ang(ib, ib, b_n)
    cb, sb = jnp.cos(th_b), jnp.sin(th_b)
    w2 = jnp.concatenate([jnp.concatenate([cb, sb], axis=1), jnp.concatenate([-sb, cb], axis=1)], axis=0)
    th_c = ang(ic, ic, D_GROUP_DIM)
    norm = 1.0 / math.sqrt(n * D_GROUP_DIM)
    return cs, tr, ti, w2, jnp.cos(th_c) * norm, jnp.sin(th_c) * norm


def fourier_mix(dz):
    n = dz.shape[0]
    a_n, b_n = FFT_A, FFT_B
    cs, tr, ti, w2, cc, sc = _dft_tables(n)
    x2 = dz.reshape(a_n, b_n * D_W)
    v = pl.pallas_call(
        _fft1_body,
        grid=(b_n,),
        in_specs=[
            pl.BlockSpec((a_n, D_W), lambda b: (0, b)),
            pl.BlockSpec((2 * a_n, a_n), lambda b: (0, 0)),
            pl.BlockSpec((1, a_n, LANES), lambda b: (b, 0, 0)),
            pl.BlockSpec((1, a_n, LANES), lambda b: (b, 0, 0)),
        ],
        out_specs=pl.BlockSpec((2, a_n, D_W), lambda b: (0, 0, b)),
        out_shape=jax.ShapeDtypeStruct((2, a_n, b_n * D_W), F32),
        compiler_params=_cparams("parallel"),
        name="fft_stage1",
    )(x2, cs, tr, ti)
    v4 = v.reshape(2, a_n, b_n, D_W)
    y = pl.pallas_call(
        _fft2_body,
        grid=(a_n,),
        in_specs=[
            pl.BlockSpec((2, 1, b_n, D_W), lambda k: (0, k, 0, 0)),
            pl.BlockSpec((2 * b_n, 2 * b_n), lambda k: (0, 0)),
            pl.BlockSpec((D_GROUP_DIM, D_GROUP_DIM), lambda k: (0, 0)),
            pl.BlockSpec((D_GROUP_DIM, D_GROUP_DIM), lambda k: (0, 0)),
        ],
        out_specs=pl.BlockSpec((b_n, D_W), lambda k: (0, k)),
        out_shape=jax.ShapeDtypeStruct((b_n, a_n * D_W), BF16),
        compiler_params=_cparams("parallel"),
        name="fft_stage2",
    )(v4, w2, cc, sc)
    return y.reshape(n, D_W)


def _router_body(x_ref, g_ref, scale_ref, shift_ref, w_ref, b_ref, hs_ref, idx_ref, wgt_ref, rank_ref, cnt_ref,
                 carry_ref):
    i = pl.program_id(0)
    tm = x_ref.shape[0]
    n_chunks = x_ref.shape[1] // LANES

    @pl.when(i == 0)
    def _():
        carry_ref[...] = jnp.zeros_like(carry_ref)

    x = x_ref[...]
    y = x * lax.rsqrt(jnp.mean(x * x, axis=-1, keepdims=True) + EPS) * g_ref[...]
    hf = y * (1.0 + scale_ref[0]) + shift_ref[0]
    for c in range(TOKEN_PITCH):
        piece = hf[:, c * LANES:(c + 1) * LANES] if c < n_chunks else jnp.zeros((tm, LANES), F32)
        hs_ref[pl.ds(c, tm, stride=TOKEN_PITCH), :] = piece
    logits = jnp.dot(hf, w_ref[0], precision=_HI, preferred_element_type=F32) + b_ref[0]
    e_iota = lax.broadcasted_iota(jnp.int32, (tm, N_EXPERTS), 1).astype(F32)
    k_iota = lax.broadcasted_iota(jnp.int32, (tm, TOP_K), 1)
    work = logits
    mask = jnp.zeros((tm, N_EXPERTS), F32)
    idx4 = jnp.zeros((tm, TOP_K), F32)
    val4 = jnp.zeros((tm, TOP_K), F32)
    sels = []
    for k in range(TOP_K):
        mk = jnp.max(work, axis=-1, keepdims=True)
        ik = jnp.min(jnp.where(work == mk, e_iota, float(N_EXPERTS)), axis=-1, keepdims=True)
        sel = e_iota == ik
        sels.append(sel)
        mask = mask + sel.astype(F32)
        idx4 = jnp.where(k_iota == k, ik, idx4)
        val4 = jnp.where(k_iota == k, mk, val4)
        work = jnp.where(sel, -jnp.inf, work)
    ex = jnp.exp(val4 - jnp.max(val4, axis=-1, keepdims=True))
    wgt_ref[...] = ex / jnp.sum(ex, axis=-1, keepdims=True)
    idx_ref[...] = idx4.astype(jnp.int32)
    r = lax.broadcasted_iota(jnp.int32, (tm, tm), 0)
    c = lax.broadcasted_iota(jnp.int32, (tm, tm), 1)
    ltri = jnp.where(r > c, 1.0, 0.0).astype(BF16)
    rank = jnp.dot(ltri, mask.astype(BF16), preferred_element_type=F32) + carry_ref[...]
    rank4 = jnp.zeros((tm, TOP_K), F32)
    for k in range(TOP_K):
        rk = jnp.sum(jnp.where(sels[k], rank, 0.0), axis=-1, keepdims=True)
        rank4 = jnp.where(k_iota == k, rk, rank4)
    rank_ref[...] = rank4.astype(jnp.int32)
    carry_ref[...] = carry_ref[...] + jnp.sum(mask, axis=0, keepdims=True)
    cnt_ref[...] = carry_ref[...]


def router(x_all, g, scale_tbl, shift_tbl, split_tile, w_router, b_router, layer, tm=ROW_TILE):
    t, d = x_all.shape
    if split_tile is None:
        sel = lambda i: (0, 0, 0)
    else:
        sel = lambda i: (jnp.where(i >= split_tile, 1, 0), 0, 0)
    return pl.pallas_call(
        _router_body,
        grid=(t // tm,),
        in_specs=[
            pl.BlockSpec((tm, d), lambda i: (i, 0)),
            pl.BlockSpec((1, d), lambda i: (0, 0)),
            pl.BlockSpec((1, 1, d), sel),
            pl.BlockSpec((1, 1, d), sel),
            pl.BlockSpec((1, d, N_EXPERTS), lambda i: (layer, 0, 0)),
            pl.BlockSpec((1, 1, N_EXPERTS), lambda i: (layer, 0, 0)),
        ],
        out_specs=[
            pl.BlockSpec((tm * TOKEN_PITCH, LANES), lambda i: (i, 0)),
            pl.BlockSpec((tm, TOP_K), lambda i: (i, 0)),
            pl.BlockSpec((tm, TOP_K), lambda i: (i, 0)),
            pl.BlockSpec((tm, TOP_K), lambda i: (i, 0)),
            pl.BlockSpec((1, N_EXPERTS), lambda i: (0, 0)),
        ],
        out_shape=[
            jax.ShapeDtypeStruct((t * TOKEN_PITCH, LANES), F32),
            jax.ShapeDtypeStruct((t, TOP_K), jnp.int32),
            jax.ShapeDtypeStruct((t, TOP_K), F32),
            jax.ShapeDtypeStruct((t, TOP_K), jnp.int32),
            jax.ShapeDtypeStruct((1, N_EXPERTS), F32),
        ],
        scratch_shapes=[pltpu.VMEM((1, N_EXPERTS), F32)],
        compiler_params=_cparams("arbitrary"),
        name="router",
    )(x_all, g[None].astype(F32), scale_tbl, shift_tbl, w_router, b_router[:, None, :])


def _row_copy(src, src_row, dst, dst_row, sem):
    return pltpu.make_async_copy(src.at[pl.ds(src_row, 1)], dst.at[pl.ds(dst_row, 1)], sem)


def _first_tile_of_expert(te_ref, i):
    return (i == 0) | (te_ref[i] != te_ref[jnp.maximum(i - 1, 0)])


def _expert_up_body(layer, te_ref, nx_ref, nu_ref, tok_cur, tok_nxt, h_hbm, w_hbm, b_ref, o_ref,
                    xbuf, wst, wbf, sems, wsem):
    i = pl.program_id(0)
    nt = pl.num_programs(0)
    n_used = nu_ref[0]
    tm = o_ref.shape[0]
    n_chunks = wst.shape[0] // LANES
    slab = tm * TOKEN_PITCH

    def gather_start(tok_ref, slot, r):
        src = pl.multiple_of(tok_ref[0, 0, r] * TOKEN_PITCH, 8)
        dst = pl.multiple_of(slot * slab + r * TOKEN_PITCH, 8)
        pltpu.make_async_copy(h_hbm.at[pl.ds(src, n_chunks)], xbuf.at[pl.ds(dst, n_chunks)], sems.at[slot]).start()

    def gather_wait(slot):
        pltpu.make_async_copy(h_hbm.at[pl.ds(0, tm * n_chunks)], xbuf.at[pl.ds(0, tm * n_chunks)],
                              sems.at[slot]).wait()

    def weight_copy(e):
        return pltpu.make_async_copy(w_hbm.at[layer, e], wst, wsem)

    @pl.when(i == 0)
    def _():
        weight_copy(te_ref[0]).start(priority=WEIGHT_DMA_PRIORITY)

        def body(r, _):
            gather_start(tok_cur, 0, r)
            return 0
        lax.fori_loop(0, tm, body, 0)

    @pl.when(i < n_used)
    def _():
        slot = i % 2
        nxt = (i + 1) % 2

        def start_next(r, _):
            gather_start(tok_nxt, nxt, r)
            return 0
        lax.fori_loop(0, tm, start_next, 0, unroll=4)
        gather_wait(slot)

        @pl.when(_first_tile_of_expert(te_ref, i))
        def _():
            weight_copy(te_ref[i]).wait()
            wbf[...] = wst[...].astype(BF16)

            @pl.when(nx_ref[i] >= 0)
            def _():
                weight_copy(nx_ref[i]).start(priority=WEIGHT_DMA_PRIORITY)

        base = slot * slab
        x = jnp.concatenate(
            [xbuf[pl.ds(base + c, tm, stride=TOKEN_PITCH), :].astype(BF16) for c in range(n_chunks)], axis=1)
        f = D_EXPERT
        cw = f // UP_CHUNKS
        bias = b_ref[0, 0]
        for c in range(UP_CHUNKS):
            gs = slice(c * cw, (c + 1) * cw)
            us = slice(f + c * cw, f + (c + 1) * cw)
            g = jnp.dot(x, wbf[:, gs], preferred_element_type=F32) + bias[:, gs]
            u = jnp.dot(x, wbf[:, us], preferred_element_type=F32) + bias[:, us]
            gate = jnp.minimum(g, SWIGLU_LIMIT)
            up = jnp.clip(u, -SWIGLU_LIMIT, SWIGLU_LIMIT)
            o_ref[:, gs] = ((up + 1.0) * (gate * jax.nn.sigmoid(SWIGLU_ALPHA * gate))).astype(o_ref.dtype)

        @pl.when(i == nt - 1)
        def _():
            gather_wait(nxt)

    @pl.when(i >= n_used)
    def _():
        @pl.when(i == n_used)
        def _():
            gather_wait(i % 2)

        o_ref[...] = jnp.zeros_like(o_ref)


def expert_up(hs, tok_tiles, tile_expert, next_expert, n_used, w_gu, b_gu, layer, tm=MOE_TM):
    nt = tok_tiles.shape[0]
    d = w_gu.shape[2]
    assert hs.shape[1] == LANES and d // LANES <= TOKEN_PITCH
    f2 = w_gu.shape[3]
    grid_spec = pltpu.PrefetchScalarGridSpec(
        num_scalar_prefetch=3,
        grid=(nt,),
        in_specs=[
            pl.BlockSpec((1, 1, tm), lambda i, *_: (i, 0, 0), memory_space=pltpu.SMEM),
            pl.BlockSpec((1, 1, tm), lambda i, *_: (jnp.minimum(i + 1, nt - 1), 0, 0), memory_space=pltpu.SMEM),
            pl.BlockSpec(memory_space=pl.ANY),
            pl.BlockSpec(memory_space=pl.ANY),
            pl.BlockSpec((1, 1, 1, f2), lambda i, te, nx, nu: (layer, te[i], 0, 0)),
        ],
        out_specs=pl.BlockSpec((tm, f2 // 2), lambda i, *_: (i, 0)),
        scratch_shapes=[
            pltpu.VMEM((2 * tm * TOKEN_PITCH, LANES), F32),
            pltpu.VMEM((d, f2), F32),
            pltpu.VMEM((d, f2), BF16),
            pltpu.SemaphoreType.DMA((2,)),
            pltpu.SemaphoreType.DMA(()),
        ],
    )
    return pl.pallas_call(
        functools.partial(_expert_up_body, layer),
        grid_spec=grid_spec,
        out_shape=jax.ShapeDtypeStruct((nt * tm, f2 // 2), BF16),
        compiler_params=_cparams("arbitrary"),
        name="expert_up",
    )(tile_expert, next_expert, n_used, tok_tiles, tok_tiles, hs, w_gu, b_gu[:, :, None, :])


def _expert_down_body(te_ref, nu_ref, dst_prev, dst_cur, a_ref, w_ref, b_ref, y_hbm, ybuf, wbf, sems):
    i = pl.program_id(0)
    n_used = nu_ref[0]
    tm = a_ref.shape[0]
    d = ybuf.shape[2]

    def scatter_start(dst_ref, slot, r, priority=0):
        _row_copy(ybuf.at[slot], r, y_hbm, dst_ref[0, 0, r], sems.at[slot]).start(priority=priority)

    def scatter_wait(slot):
        pltpu.make_async_copy(ybuf.at[slot], y_hbm.at[pl.ds(0, tm)], sems.at[slot]).wait()

    def compute(slot, prev, scatter_prev):
        a = a_ref[...]
        cw = d // DOWN_CHUNKS
        per = -(-tm // DOWN_CHUNKS)
        for c in range(DOWN_CHUNKS):
            if scatter_prev:
                for r in range(c * per, min(tm, (c + 1) * per)):
                    scatter_start(dst_prev, prev, r, priority=r % 2)
            cs = slice(c * cw, (c + 1) * cw)
            ybuf[slot, :, cs] = jnp.dot(a, wbf[:, cs], preferred_element_type=F32) + b_ref[0, 0][:, cs]

    @pl.when(i < n_used)
    def _():
        slot = i % 2
        prev = (i + 1) % 2

        @pl.when(i >= 2)
        def _():
            scatter_wait(slot)

        @pl.when(_first_tile_of_expert(te_ref, i))
        def _():
            wbf[...] = w_ref[0, 0].astype(BF16)

        @pl.when(i == 0)
        def _():
            compute(slot, prev, False)

        @pl.when(i > 0)
        def _():
            compute(slot, prev, True)

        @pl.when(i == n_used - 1)
        def _():
            def body(r, _):
                scatter_start(dst_cur, slot, r)
                return 0
            lax.fori_loop(0, tm, body, 0)

            @pl.when(i >= 1)
            def _():
                scatter_wait(prev)

            scatter_wait(slot)

    @pl.when(i >= n_used)
    def _():
        @pl.when(i == n_used)
        def _():
            ybuf[0] = jnp.zeros((tm, d), F32)

        zero_fill = pltpu.make_async_copy(ybuf.at[0], y_hbm.at[pl.ds(pl.multiple_of(i * tm, tm), tm)], sems.at[0])
        zero_fill.start()
        zero_fill.wait()


def expert_down(act, dst_tiles, tile_expert, n_used, w_down, b_down, layer, tm=MOE_TM):
    p, f = act.shape
    nt = p // tm
    d = w_down.shape[3]
    grid_spec = pltpu.PrefetchScalarGridSpec(
        num_scalar_prefetch=2,
        grid=(nt,),
        in_specs=[
            pl.BlockSpec((1, 1, tm), lambda i, te, nu: (jnp.maximum(i - 1, 0), 0, 0), memory_space=pltpu.SMEM),
            pl.BlockSpec((1, 1, tm), lambda i, te, nu: (i, 0, 0), memory_space=pltpu.SMEM),
            pl.BlockSpec((tm, f), lambda i, te, nu: (i, 0)),
            pl.BlockSpec((1, 1, f, d), lambda i, te, nu: (layer, te[i], 0, 0)),
            pl.BlockSpec((1, 1, 1, d), lambda i, te, nu: (layer, te[i], 0, 0)),
        ],
        out_specs=pl.BlockSpec(memory_space=pl.ANY),
        scratch_shapes=[pltpu.VMEM((2, tm, d), F32), pltpu.VMEM((f, d), BF16), pltpu.SemaphoreType.DMA((2,))],
    )
    return pl.pallas_call(
        _expert_down_body,
        grid_spec=grid_spec,
        out_shape=jax.ShapeDtypeStruct((p, d), F32),
        compiler_params=_cparams("arbitrary"),
        name="expert_down",
    )(tile_expert, n_used, dst_tiles, dst_tiles, act, w_down, b_down[:, :, None, :])


def _combine_body(wgt_ref, x_ref, gate_ref, y0_ref, y1_ref, y2_ref, y3_ref, o_ref):
    w = wgt_ref[...]
    f = w[:, 0:1] * y0_ref[...]
    for k, y_ref in enumerate((y1_ref, y2_ref, y3_ref), start=1):
        f = f + w[:, k:k + 1] * y_ref[...]
    o_ref[...] = x_ref[...] + gate_ref[0] * f


def combine(x_all, y_planes, wgt, gate_tbl, split_tile, tb=COMBINE_TB):
    t, d = x_all.shape
    nsteps = t // tb
    if split_tile is None:
        sel = lambda i: (0, 0, 0)
    else:
        sel = lambda i: (jnp.where(i >= split_tile, 1, 0), 0, 0)
    plane = lambda k: pl.BlockSpec((tb, d), lambda i: (k * nsteps + i, 0))
    return pl.pallas_call(
        _combine_body,
        grid=(nsteps,),
        in_specs=[
            pl.BlockSpec((tb, TOP_K), lambda i: (i, 0)),
            pl.BlockSpec((tb, d), lambda i: (i, 0)),
            pl.BlockSpec((1, 1, d), sel),
        ] + [plane(k) for k in range(TOP_K)],
        out_specs=pl.BlockSpec((tb, d), lambda i: (i, 0)),
        out_shape=jax.ShapeDtypeStruct((t, d), F32),
        compiler_params=_cparams("parallel"),
        name="moe_combine",
    )(wgt, x_all, gate_tbl, *([y_planes] * TOP_K))


def _routing_tables(idx4, rank4, counts, t, tm, nt):
    i32 = jnp.int32
    e_ids = jnp.arange(N_EXPERTS, dtype=i32)
    cnt = counts[0].astype(i32)
    tiles_e = (cnt + tm - 1) // tm
    tile_end = jnp.cumsum(tiles_e)
    offsets = (tile_end - tiles_e) * tm
    n_used = tile_end[-1]
    slot4 = jnp.sum(jnp.where(idx4[:, :, None] == e_ids, offsets, 0), axis=-1) + rank4
    te = jnp.sum((tile_end[None, :] <= jnp.arange(nt, dtype=i32)[:, None]).astype(i32), axis=1)
    last_e = jnp.sum(jnp.where(jnp.arange(nt, dtype=i32) == n_used - 1, te, 0))
    te = jnp.minimum(te, last_e)
    later_used = (tiles_e[None, :] > 0) & (e_ids[None, :] > e_ids[:, None])
    nxt_e = jnp.min(jnp.where(later_used, e_ids[None, :], N_EXPERTS), axis=1)
    nxt_e = jnp.where(nxt_e >= N_EXPERTS, -1, nxt_e)
    nx = jnp.sum(jnp.where(te[:, None] == e_ids, nxt_e, 0), axis=1).astype(i32)
    assign = jnp.arange(t * TOP_K, dtype=i32)
    code = jnp.full((nt * tm,), -1, i32).at[slot4.reshape(-1)].set(assign)
    is_pad = code < 0
    tok = jnp.where(is_pad, 0, code // TOP_K)
    pad_row = TOP_K * t + jnp.cumsum(is_pad.astype(i32)) - 1
    dst = jnp.where(is_pad, pad_row, (code % TOP_K) * t + code // TOP_K)
    return tok, dst, te.astype(i32), nx, n_used.astype(i32).reshape(1)


def moe_block(x_all, g_ffn, scale_tbl, shift_tbl, gate_tbl, split_rows, w_router, b_router, w_gu, b_gu,
              w_down, b_down, layer):
    t, d = x_all.shape
    tm = MOE_TM
    hs, idx4, wgt4, rank4, counts = router(x_all, g_ffn, scale_tbl, shift_tbl,
                                           None if split_rows is None else split_rows // ROW_TILE,
                                           w_router, b_router, layer)
    nt = (t * TOP_K + N_EXPERTS * (tm - 1) + tm - 1) // tm
    tok, dst, te, nx, n_used = _routing_tables(idx4, rank4, counts, t, tm, nt)
    act = expert_up(hs, tok.reshape(nt, 1, tm), te, nx, n_used, w_gu, b_gu, layer)
    y_planes = expert_down(act, dst.reshape(nt, 1, tm), te, n_used, w_down, b_down, layer)
    return combine(x_all, y_planes, wgt4, gate_tbl, None if split_rows is None else split_rows // COMBINE_TB)


def _rope_angles(n_tokens, rot_dim):
    rows = n_tokens // GRID_W
    row = jnp.repeat(jnp.arange(rows, dtype=F32), GRID_W)
    col = jnp.tile(jnp.arange(GRID_W, dtype=F32), rows)
    n_freq = rot_dim // 4
    inv_freq = ROPE_THETA ** (-jnp.arange(n_freq, dtype=F32) / n_freq)
    ang = jnp.concatenate([row[:, None] * inv_freq, col[:, None] * inv_freq], axis=-1)
    return jnp.cos(ang), jnp.sin(ang)


def _rope_tables(n, n_ctx):
    cos, sin = _rope_angles(n, 64)
    z = jnp.zeros_like(cos)
    pair = (jnp.concatenate([cos] * 4, axis=1), jnp.concatenate([-sin, sin, -sin, sin], axis=1))
    single = (jnp.concatenate([cos, cos, z, z], axis=1), jnp.concatenate([-sin, sin, z, z], axis=1))
    ident = (jnp.ones((n_ctx, LANES), F32), jnp.zeros((n_ctx, LANES), F32))
    return pair, single, ident


def _pad_heads(w, n_heads, widths, pad_to):
    k = w.shape[0]
    per = sum(widths)
    w3 = w.reshape(k, n_heads, per)[:, :, :widths[0]]
    return jnp.pad(w3, ((0, 0), (0, 0), (0, pad_to - widths[0]))).reshape(k, n_heads * pad_to)


def _kv_weight(w_ukv_l):
    kin = C_KV_RANK + 2 * C_ROPE
    w_kv3 = w_ukv_l.reshape(C_KV_RANK, C_HEADS, C_NOPE + C_HEAD_V)
    w_k = jnp.pad(w_kv3[:, :, :C_NOPE], ((0, 0), (0, 0), (0, C_QK_PAD - C_NOPE)))
    rope_rows = jnp.pad(jnp.eye(C_ROPE, dtype=F32), ((0, 0), (C_NOPE, C_QK_PAD - C_NOPE - C_ROPE)))
    rope_rows = jnp.broadcast_to(rope_rows[:, None, :], (C_ROPE, C_HEADS, C_QK_PAD))
    w_k = jnp.concatenate([w_k, rope_rows, jnp.zeros((C_ROPE, C_HEADS, C_QK_PAD), F32)], axis=0)
    w_v = jnp.pad(w_kv3[:, :, C_NOPE:], ((0, 2 * C_ROPE), (0, 0), (0, 0)))
    return jnp.concatenate([w_k.reshape(kin, -1), w_v.reshape(kin, -1)], axis=1).astype(BF16)


def _tbl(lat_row, ctx_row=None):
    rows = [lat_row] if ctx_row is None else [lat_row, ctx_row]
    return jnp.stack(rows)[:, None, :]


def kernel(x, c, ctx, c_ctx, w_mod, b_mod, g_mix, g_ffn, w_in_ab, ws_a, bs_a, sink_b, w_out_ab, w_in_cd, g_cq,
           w_uq, g_ckv, w_ukv, w_out_cd, w_router, b_router, w_gu, b_gu, w_down, b_down, g_final):
    bsz, n, d = x.shape
    n_ctx = ctx.shape[1]
    assert bsz == 1 and d == D_MODEL and n == FFT_A * FFT_B
    xl = x[0]
    xc = ctx[0]
    c_rows = jnp.zeros((8, d), F32).at[0].set(c[0]).at[1].set(c_ctx)
    pair_t, single_t, ident_t = _rope_tables(n, n_ctx)
    zeros_tbl = jnp.zeros((1, 1, d), F32)

    def mod_rows(l):
        mod = modulation(c_rows, w_mod, b_mod, l)
        lat = [mod[0, k * d:(k + 1) * d] for k in range(N_MOD)]
        cx = [mod[1, k * d:(k + 1) * d] for k in range(N_MOD)]
        return lat, cx

    lat, cx = mod_rows(0)
    h = norm_mod(xl, g_mix[0], _tbl(lat[1]), _tbl(lat[0]), out_dtype=BF16, name="norm_mix0")
    hc = norm_mod(xc, g_mix[0], _tbl(cx[1]), _tbl(cx[0]), out_dtype=BF16, name="norm_mix0_ctx")
    w_in = w_in_ab[0].astype(BF16)
    z = matmul([h], [w_in], out_dtype=BF16, tm=1024, tn=512, a_resident=True, name="proj_ab")
    zc = matmul([hc], [w_in], out_dtype=BF16, tm=n_ctx, tn=512, a_resident=True, name="proj_ab_ctx")
    ws_bf = ws_a[0].astype(BF16)
    bs_col = bs_a[0][:, :, None]
    a_lat = gmlp(z, ws_bf, bs_col)
    a_ctx = gmlp(zc, ws_bf, bs_col)
    q_lat = rope_q(z, *pair_t)
    q_ctx = rope_q(zc, *ident_t)
    kd, vd = rope_kv(z, *pair_t)
    kd_c, vd_c = rope_kv(zc, *ident_t)
    sink4 = sink_b[0].astype(F32).reshape(B_KV_HEADS, B_GROUP, 1, 1)
    b_lat = window_attn(q_lat, kd, vd, kd_c, vd_c, sink4, True)
    b_ctx = window_attn(q_ctx, None, None, kd_c, vd_c, sink4, False)
    w_out = w_out_ab[0].astype(BF16)
    xl = matmul([a_lat, b_lat], [w_out[:A_W], w_out[A_W:]], out_dtype=F32, tm=512, tn=1024,
                res=xl, gate=lat[2][None], name="out_ab")
    xc = matmul([a_ctx, b_ctx], [w_out[:A_W], w_out[A_W:]], out_dtype=F32, tm=n_ctx, tn=1024,
                res=xc, gate=cx[2][None], name="out_ab_ctx")
    x_all = jnp.concatenate([xl, xc], axis=0)
    x_all = moe_block(x_all, g_ffn[0], _tbl(lat[4], cx[4]), _tbl(lat[3], cx[3]), _tbl(lat[5], cx[5]), n,
                      w_router, b_router, w_gu, b_gu, w_down, b_down, 0)
    xl = x_all
    xc = x_all[n:]

    lat, cx = mod_rows(1)
    h = norm_mod(xl, g_mix[1], _tbl(lat[1]), _tbl(lat[0]), out_dtype=BF16, rows=n, name="norm_mix1")
    hc = norm_mod(xc, g_mix[1], _tbl(cx[1]), _tbl(cx[0]), out_dtype=BF16, name="norm_mix1_ctx")
    w_in = w_in_cd[0]
    w_cq = w_in[:, :CD_KV0].astype(BF16)
    w_kvr = jnp.pad(w_in[:, CD_KV0:CD_D0], ((0, 0), (0, 64))).astype(BF16)
    w_dz = w_in[:, CD_D0:].astype(BF16)
    cq = matmul([h], [w_cq], out_dtype=BF16, tm=1024, tn=C_Q_RANK, name="proj_cq")
    ckvkr = matmul([h], [w_kvr], out_dtype=BF16, tm=1024, tn=640, name="proj_ckv")
    ckvkr_c = matmul([hc], [w_kvr], out_dtype=BF16, tm=n_ctx, tn=640, name="proj_ckv_ctx")
    dz = matmul([h], [w_dz], out_dtype=F32, tm=1024, tn=D_W, name="proj_dz")
    cqn = norm_mod(cq, g_cq[0], jnp.zeros((1, 1, C_Q_RANK), F32), jnp.zeros((1, 1, C_Q_RANK), F32),
                   out_dtype=BF16, name="norm_cq")
    w_uq_p = _pad_heads(w_uq[0], C_HEADS, (C_NOPE + C_ROPE,), C_QK_PAD).astype(BF16)
    qp = matmul([cqn], [w_uq_p], out_dtype=BF16, tm=1024, tn=1024, name="proj_uq")
    kv_in = kvprep(ckvkr, g_ckv[0], *single_t)
    kv_in_c = kvprep(ckvkr_c, g_ckv[0], *ident_t)
    w_kv = _kv_weight(w_ukv[0])
    kv = matmul([kv_in], [w_kv], out_dtype=BF16, tm=1024, tn=1024, name="proj_ukv")
    kv_c = matmul([kv_in_c], [w_kv], out_dtype=BF16, tm=n_ctx, tn=1024, name="proj_ukv_ctx")
    c_lat = mla_attn(qp, kv, kv_c, *single_t)
    d_lat = fourier_mix(dz)
    w_out = w_out_cd[0].astype(BF16)
    n_c = C_HEADS * C_HEAD_V
    xl = matmul([c_lat, d_lat], [w_out[:n_c], w_out[n_c:]], out_dtype=F32, tm=512, tn=1024,
                res=xl, gate=lat[2][None], name="out_cd")
    xl = moe_block(xl, g_ffn[1], _tbl(lat[4]), _tbl(lat[3]), _tbl(lat[5]), None,
                   w_router, b_router, w_gu, b_gu, w_down, b_down, 1)
    out = norm_mod(xl, g_final, zeros_tbl, zeros_tbl, out_dtype=F32, name="norm_final")
    return out[None]
```
